```python
import math
import jax
import jax.numpy as jnp
from jax import lax
import numpy as np

D_MODEL = 2048
BATCH = 2
SEQ = 8192
DEPTH = 4

N_MIXERS = 3
BLOCK = 128
LN_EPS = 1e-5
ALPHA = (2 * DEPTH) ** 0.25
BETA = (8 * DEPTH) ** -0.25

DIFF_HEADS = 16
DIFF_HEAD_DIM = 64
FOX_HEADS = 16
FOX_HEAD_DIM = 128
SWA_Q_HEADS = 32
SWA_KV_HEADS = 4
SWA_GROUP = SWA_Q_HEADS // SWA_KV_HEADS
SWA_HEAD_DIM = 64
WINDOW = 128
SWA_Q_DIM = SWA_Q_HEADS * SWA_HEAD_DIM
SWA_KV_DIM = SWA_KV_HEADS * SWA_HEAD_DIM

D_FF_DENSE = 5632
N_EXPERTS = 8
TOP_K = 2
D_FF_EXPERT = 7168

N_DIFF = (DEPTH + 2) // 3
N_FOX = (DEPTH + 1) // 3
N_SWA = DEPTH // 3
N_DENSE = (DEPTH + 1) // 2
N_MOE = DEPTH // 2

DIFF_IN = 3 * D_MODEL
FOX_IN = 4 * D_MODEL + FOX_HEADS
SWA_IN = SWA_Q_DIM + 2 * SWA_KV_DIM

kernel_name = 'hybrid_diff_fox_swa_moe_deepnorm'

F32 = jnp.float32


def _alibi_slopes(n):
    def pow2(m):
        start = 2.0 ** (-8.0 / m)
        return [start ** (i + 1) for i in range(m)]
    if n & (n - 1) == 0:
        s = pow2(n)
    else:
        c = 2 ** int(math.floor(math.log2(n)))
        s = pow2(c) + pow2(2 * c)[0::2][: n - c]
    return jnp.asarray(np.array(s, dtype=np.float32))


def _layer_norm(x, g, b):
    xf = x.astype(F32)
    mu = jnp.mean(xf, axis=-1, keepdims=True)
    xc = xf - mu
    var = jnp.mean(xc * xc, axis=-1, keepdims=True)
    y = xc * lax.rsqrt(var + LN_EPS) * g.astype(F32) + b.astype(F32)
    return y.astype(x.dtype)


def _rms_norm(x, g):
    xf = x.astype(F32)
    y = xf * lax.rsqrt(jnp.mean(xf * xf, axis=-1, keepdims=True) + LN_EPS) * g.astype(F32)
    return y.astype(x.dtype)


def _swiglu(h, w_gate, w_up, w_down):
    return (jax.nn.silu(h @ w_gate) * (h @ w_up)) @ w_down


def differential_attention(h, w_in, lam_params, subln_gain, w_out, layer_idx):
    B, S, _ = h.shape
    H, d = DIFF_HEADS, DIFF_HEAD_DIM
    proj = h @ w_in
    q = proj[..., :D_MODEL].reshape(B, S, H, 2, d)
    k = proj[..., D_MODEL:2 * D_MODEL].reshape(B, S, H, 2, d)
    v = proj[..., 2 * D_MODEL:].reshape(B, S, H, 2 * d)
    lam_init = 0.8 - 0.6 * math.exp(-0.3 * layer_idx)
    lp = lam_params.astype(F32)
    lam = jnp.exp(jnp.sum(lp[0] * lp[1])) - jnp.exp(jnp.sum(lp[2] * lp[3])) + lam_init
    slopes = _alibi_slopes(H)
    scale = d ** -0.5

    def block(q0):
        q1 = q0 + BLOCK
        s = jnp.einsum('bqhcd,bkhcd->bhcqk', q[:, q0:q1], k[:, :q1]).astype(F32) * scale
        dist = (jnp.arange(q0, q1)[:, None] - jnp.arange(q1)[None, :]).astype(F32)
        s = jnp.where(dist >= 0, s - slopes[:, None, None, None] * dist, -jnp.inf)
        p = jax.nn.softmax(s, axis=-1)
        a = p[:, :, 0] - lam * p[:, :, 1]
        return jnp.einsum('bhqk,bkhe->bqhe', a.astype(v.dtype), v[:, :q1])

    o = jnp.concatenate([block(i * BLOCK) for i in range(S // BLOCK)], axis=1)
    o = _rms_norm(o, subln_gain) * (1.0 - lam_init)
    return o.reshape(B, S, D_MODEL) @ w_out


def forgetting_attention(h, w_in, b_f, qk_gain, w_out):
    B, S, _ = h.shape
    H, dh = FOX_HEADS, FOX_HEAD_DIM
    proj = h @ w_in
    q = _rms_norm(proj[..., :D_MODEL].reshape(B, S, H, dh), qk_gain[0])
    k = _rms_norm(proj[..., D_MODEL:2 * D_MODEL].reshape(B, S, H, dh), qk_gain[1])
    v = proj[..., 2 * D_MODEL:3 * D_MODEL].reshape(B, S, H, dh)
    f_logit = proj[..., 3 * D_MODEL:3 * D_MODEL + H]
    g = proj[..., 3 * D_MODEL + H:]
    log_f = jax.nn.log_sigmoid(f_logit.astype(F32) + b_f.astype(F32))
    c = jnp.cumsum(log_f, axis=1).transpose(0, 2, 1)
    scale = dh ** -0.5

    def block(q0):
        q1 = q0 + BLOCK
        s = jnp.einsum('bqhd,bkhd->bhqk', q[:, q0:q1], k[:, :q1]).astype(F32) * scale
        s = s + c[:, :, q0:q1, None] - c[:, :, None, :q1]
        causal = jnp.arange(q0, q1)[:, None] >= jnp.arange(q1)[None, :]
        s = jnp.where(causal, s, -jnp.inf)
        p = jax.nn.softmax(s, axis=-1)
        return jnp.einsum('bhqk,bkhd->bqhd', p.astype(v.dtype), v[:, :q1])

    o = jnp.concatenate([block(i * BLOCK) for i in range(S // BLOCK)], axis=1)
    o = o.reshape(B, S, D_MODEL) * jax.nn.sigmoid(g)
    return o @ w_out


def sliding_window_gqa(h, w_in, sinks, w_out):
    B, S, _ = h.shape
    nb = S // BLOCK
    Hk, G, d = SWA_KV_HEADS, SWA_GROUP, SWA_HEAD_DIM
    proj = h @ w_in
    q = proj[..., :SWA_Q_DIM].reshape(B, nb, BLOCK, Hk, G, d)
    k = proj[..., SWA_Q_DIM:SWA_Q_DIM + SWA_KV_DIM].reshape(B, nb, BLOCK, Hk, d)
    v = proj[..., SWA_Q_DIM + SWA_KV_DIM:].reshape(B, nb, BLOCK, Hk, d)

    def with_prev(t):
        prev = jnp.pad(t[:, :-1], ((0, 0), (1, 0), (0, 0), (0, 0), (0, 0)))
        return jnp.concatenate([prev, t], axis=2)

    kk, vv = with_prev(k), with_prev(v)
    s = jnp.einsum('bnqhgd,bnkhd->bnhgqk', q, kk).astype(F32) * (d ** -0.5)
    qi = jnp.arange(BLOCK)[:, None]
    kj = jnp.arange(2 * BLOCK)[None, :]
    dist = qi - kj + BLOCK
    blk = jnp.arange(nb)[:, None, None]
    valid = (dist >= 0) & (dist < WINDOW) & (blk * BLOCK - BLOCK + kj >= 0)
    slopes = _alibi_slopes(SWA_Q_HEADS).reshape(Hk, G, 1, 1)
    s = jnp.where(valid[:, None, None], s - slopes * dist.astype(F32), -jnp.inf)
    sink = jnp.broadcast_to(sinks.astype(F32).reshape(1, 1, Hk, G, 1, 1), s.shape[:-1] + (1,))
    p = jax.nn.softmax(jnp.concatenate([s, sink], axis=-1), axis=-1)[..., :-1]
    o = jnp.einsum('bnhgqk,bnkhd->bnqhgd', p.astype(vv.dtype), vv).reshape(B, S, SWA_Q_DIM)
    return o @ w_out


def moe_swiglu(h, router, w_gate, w_up, w_down):
    logits = (h @ router).astype(F32)
    top_v, top_i = lax.top_k(logits, TOP_K)
    gates = jax.nn.softmax(top_v, axis=-1)
    combine = jnp.sum(jax.nn.one_hot(top_i, N_EXPERTS, dtype=F32) * gates[..., None], axis=-2)
    y = jnp.zeros_like(h)
    for e in range(N_EXPERTS):
        y = y + combine[..., e:e + 1].astype(h.dtype) * _swiglu(h, w_gate[e], w_up[e], w_down[e])
    return y


def setup_inputs(seed: int = 0) -> dict:
    key = jax.random.key(seed)
    ks = jax.random.split(key, 32)

    def nrm(k, shape, scale):
        return jax.random.normal(k, shape, F32) * scale

    D = D_MODEL
    s_d = D ** -0.5
    x = nrm(ks[0], (BATCH, SEQ, D), 1.0)
    ln_gain = 1.0 + nrm(ks[1], (DEPTH, 2, D), 0.02)
    ln_bias = nrm(ks[2], (DEPTH, 2, D), 0.02)

    diff_w_in = jnp.concatenate([nrm(ks[3], (N_DIFF, D, 2 * D), s_d),
                                 nrm(ks[4], (N_DIFF, D, D), s_d * BETA)], axis=-1)
    diff_lambda = nrm(ks[5], (N_DIFF, 4, DIFF_HEAD_DIM), 0.1)
    diff_subln_gain = 1.0 + nrm(ks[6], (N_DIFF, 2 * DIFF_HEAD_DIM), 0.02)
    diff_w_out = nrm(ks[7], (N_DIFF, D, D), s_d * BETA)

    fox_w_in = jnp.concatenate([nrm(ks[8], (N_FOX, D, 2 * D), s_d),
                                nrm(ks[9], (N_FOX, D, D), s_d * BETA),
                                nrm(ks[10], (N_FOX, D, FOX_HEADS + D), s_d)], axis=-1)
    fox_b_f = jnp.linspace(1.0, 5.0, FOX_HEADS, dtype=F32)[None] + nrm(ks[11], (N_FOX, FOX_HEADS), 0.1)
    fox_qk_gain = 1.0 + nrm(ks[12], (N_FOX, 2, FOX_HEAD_DIM), 0.02)
    fox_w_out = nrm(ks[13], (N_FOX, D, D), s_d * BETA)

    swa_w_in = jnp.concatenate([nrm(ks[14], (N_SWA, D, SWA_Q_DIM + SWA_KV_DIM), s_d),
                                nrm(ks[15], (N_SWA, D, SWA_KV_DIM), s_d * BETA)], axis=-1)
    swa_sinks = nrm(ks[16], (N_SWA, SWA_Q_HEADS), 0.5)
    swa_w_out = nrm(ks[17], (N_SWA, SWA_Q_DIM, D), SWA_Q_DIM ** -0.5 * BETA)

    dense_w_gate = nrm(ks[18], (N_DENSE, D, D_FF_DENSE), s_d * BETA)
    dense_w_up = nrm(ks[19], (N_DENSE, D, D_FF_DENSE), s_d * BETA)
    dense_w_down = nrm(ks[20], (N_DENSE, D_FF_DENSE, D), D_FF_DENSE ** -0.5 * BETA)

    moe_router = nrm(ks[21], (N_MOE, D, N_EXPERTS), s_d)
    moe_w_gate = nrm(ks[22], (N_MOE, N_EXPERTS, D, D_FF_EXPERT), s_d * BETA)
    moe_w_up = nrm(ks[23], (N_MOE, N_EXPERTS, D, D_FF_EXPERT), s_d * BETA)
    moe_w_down = nrm(ks[24], (N_MOE, N_EXPERTS, D_FF_EXPERT, D), D_FF_EXPERT ** -0.5 * BETA)

    return {'x': x, 'ln_gain': ln_gain, 'ln_bias': ln_bias,
            'diff_w_in': diff_w_in, 'diff_lambda': diff_lambda,
            'diff_subln_gain': diff_subln_gain, 'diff_w_out': diff_w_out,
            'fox_w_in': fox_w_in, 'fox_b_f': fox_b_f, 'fox_qk_gain': fox_qk_gain,
            'fox_w_out': fox_w_out,
            'swa_w_in': swa_w_in, 'swa_sinks': swa_sinks, 'swa_w_out': swa_w_out,
            'dense_w_gate': dense_w_gate, 'dense_w_up': dense_w_up, 'dense_w_down': dense_w_down,
            'moe_router': moe_router, 'moe_w_gate': moe_w_gate, 'moe_w_up': moe_w_up,
            'moe_w_down': moe_w_down}


def reference(x, ln_gain, ln_bias, diff_w_in, diff_lambda, diff_subln_gain, diff_w_out,
              fox_w_in, fox_b_f, fox_qk_gain, fox_w_out, swa_w_in, swa_sinks, swa_w_out,
              dense_w_gate, dense_w_up, dense_w_down, moe_router, moe_w_gate, moe_w_up,
              moe_w_down):
    h = x
    for i in range(DEPTH):
        kind, j = i % N_MIXERS, i // N_MIXERS
        if kind == 0:
            mix = differential_attention(h, diff_w_in[j], diff_lambda[j], diff_subln_gain[j],
                                         diff_w_out[j], i)
        elif kind == 1:
            mix = forgetting_attention(h, fox_w_in[j], fox_b_f[j], fox_qk_gain[j], fox_w_out[j])
        else:
            mix = sliding_window_gqa(h, swa_w_in[j], swa_sinks[j], swa_w_out[j])
        h = _layer_norm(ALPHA * h + mix, ln_gain[i, 0], ln_bias[i, 0])
        c = i // 2
        if i % 2 == 0:
            ffn = _swiglu(h, dense_w_gate[c], dense_w_up[c], dense_w_down[c])
        else:
            ffn = moe_swiglu(h, moe_router[c], moe_w_gate[c], moe_w_up[c], moe_w_down[c])
        h = _layer_norm(ALPHA * h + ffn, ln_gain[i, 1], ln_bias[i, 1])
    return h
```

```python
import functools
import math

import numpy as np
import jax
import jax.numpy as jnp
from jax import lax
from jax.experimental import pallas as pl
from jax.experimental.pallas import tpu as pltpu

F32 = jnp.float32
BF16 = jnp.bfloat16

D_MODEL = 2048
DEPTH = 4
LN_EPS = 1e-5
ALPHA = (2 * DEPTH) ** 0.25

DIFF_HEADS = 16
DIFF_HEAD_DIM = 64
FOX_HEADS = 16
FOX_HEAD_DIM = 128
SWA_Q_HEADS = 32
SWA_KV_HEADS = 4
SWA_HEAD_DIM = 64
SWA_BLOCK = 128
SWA_Q_DIM = SWA_Q_HEADS * SWA_HEAD_DIM
SWA_KV_DIM = SWA_KV_HEADS * SWA_HEAD_DIM
N_EXPERTS = 8
TOP_K = 2

LANES = 128
VMEM_LIMIT = 56 * 1024 * 1024

NEG_INF = float("-inf")
_NT = (((1,), (1,)), ((), ()))


def _slopes(n):
    def pow2(m):
        start = 2.0 ** (-8.0 / m)
        return [start ** (i + 1) for i in range(m)]
    if n & (n - 1) == 0:
        s = pow2(n)
    else:
        c = 2 ** int(math.floor(math.log2(n)))
        s = pow2(c) + pow2(2 * c)[0::2][: n - c]
    return jnp.asarray(np.array(s, dtype=np.float32))


def _params(sem, vmem=VMEM_LIMIT):
    return pltpu.CompilerParams(dimension_semantics=sem, vmem_limit_bytes=vmem)


def _layer_norm_rows(y, g, b):
    mu = jnp.mean(y, axis=-1, keepdims=True)
    yc = y - mu
    var = jnp.mean(yc * yc, axis=-1, keepdims=True)
    return yc * lax.rsqrt(var + LN_EPS) * g + b


def _sigmoid(x):
    return 1.0 / (1.0 + jnp.exp(-x))


def _proj_kernel(a_ref, w_ref, g_ref, o_ref, *, n_norm_tiles, tn):
    acc = jnp.dot(a_ref[...], w_ref[...].astype(BF16), preferred_element_type=F32)
    if n_norm_tiles == 0:
        o_ref[...] = acc.astype(o_ref.dtype)
        return
    j = pl.program_id(1)

    @pl.when(j < n_norm_tiles)
    def _():
        for c in range(tn // LANES):
            sl = slice(c * LANES, (c + 1) * LANES)
            blk = acc[:, sl]
            ms = jnp.mean(blk * blk, axis=-1, keepdims=True)
            o_ref[:, sl] = (blk * lax.rsqrt(ms + LN_EPS) * g_ref[:, sl]).astype(o_ref.dtype)

    @pl.when(j >= n_norm_tiles)
    def _():
        o_ref[...] = acc.astype(o_ref.dtype)


def _proj(a, w, n_out, *, gain=None, n_norm=0, out_dtype=BF16, tm=1024, tn=512):
    m, k = a.shape
    tm = min(tm, m)
    tn = min(tn, n_out)
    if gain is None:
        gain = jnp.ones((1, n_out), F32)
    kern = functools.partial(_proj_kernel, n_norm_tiles=n_norm // tn, tn=tn)
    return pl.pallas_call(
        kern,
        grid=(m // tm, n_out // tn),
        in_specs=[pl.BlockSpec((tm, k), lambda i, j: (i, 0)),
                  pl.BlockSpec((k, tn), lambda i, j: (0, j)),
                  pl.BlockSpec((1, tn), lambda i, j: (0, j))],
        out_specs=pl.BlockSpec((tm, tn), lambda i, j: (i, j)),
        out_shape=jax.ShapeDtypeStruct((m, n_out), out_dtype),
        compiler_params=_params(("parallel", "arbitrary")),
    )(a, w, gain)


def _outproj_ln_kernel(o_ref, w_ref, h_ref, g_ref, b_ref, hf_ref, hb_ref, acc_ref):
    k = pl.program_id(1)

    @pl.when(k == 0)
    def _():
        acc_ref[...] = jnp.zeros_like(acc_ref)

    acc_ref[...] += jnp.dot(o_ref[...], w_ref[...].astype(BF16), preferred_element_type=F32)

    @pl.when(k == pl.num_programs(1) - 1)
    def _():
        out = _layer_norm_rows(ALPHA * h_ref[...] + acc_ref[...], g_ref[...], b_ref[...])
        hf_ref[...] = out
        hb_ref[...] = out.astype(BF16)


def _outproj_ln(o, w, h, g, b, *, tm=512, tk=512):
    m, k = o.shape
    n = w.shape[1]
    tm = min(tm, m)
    return pl.pallas_call(
        _outproj_ln_kernel,
        grid=(m // tm, k // tk),
        in_specs=[pl.BlockSpec((tm, tk), lambda i, kk: (i, kk)),
                  pl.BlockSpec((tk, n), lambda i, kk: (kk, 0)),
                  pl.BlockSpec((tm, n), lambda i, kk: (i, 0)),
                  pl.BlockSpec((1, n), lambda i, kk: (0, 0)),
                  pl.BlockSpec((1, n), lambda i, kk: (0, 0))],
        out_specs=[pl.BlockSpec((tm, n), lambda i, kk: (i, 0)),
                   pl.BlockSpec((tm, n), lambda i, kk: (i, 0))],
        out_shape=[jax.ShapeDtypeStruct((m, n), F32), jax.ShapeDtypeStruct((m, n), BF16)],
        scratch_shapes=[pltpu.VMEM((tm, n), F32)],
        compiler_params=_params(("parallel", "arbitrary")),
    )(o, w, h, g, b)


def _res_ln_kernel(y_ref, h_ref, g_ref, b_ref, hf_ref, hb_ref):
    out = _layer_norm_rows(ALPHA * h_ref[...] + y_ref[...], g_ref[...], b_ref[...])
    hf_ref[...] = out
    hb_ref[...] = out.astype(BF16)


def _res_ln(y, h, g, b, *, tm=512):
    m, n = h.shape
    tm = min(tm, m)
    row = pl.BlockSpec((tm, n), lambda i: (i, 0))
    vec = pl.BlockSpec((1, n), lambda i: (0, 0))
    return pl.pallas_call(
        _res_ln_kernel,
        grid=(m // tm,),
        in_specs=[row, row, vec, vec],
        out_specs=[row, row],
        out_shape=[jax.ShapeDtypeStruct((m, n), F32), jax.ShapeDtypeStruct((m, n), BF16)],
        compiler_params=_params(("parallel",)),
    )(y, h, g, b)


def _diff_attn_kernel(slopes_ref, q_ref, k_ref, v_ref, lam_ref, gain_ref, o_ref, *, tq, lam_init):
    h = pl.program_id(1)
    qi = pl.program_id(2)
    slope = slopes_ref[h]
    half = DIFF_HEAD_DIM

    q = q_ref[...] * jnp.asarray(DIFF_HEAD_DIM ** -0.5, BF16)
    lane = lax.broadcasted_iota(jnp.int32, (tq, LANES), 1)
    zero = jnp.zeros_like(q)
    qs = jnp.concatenate([jnp.where(lane < half, q, zero), jnp.where(lane < half, zero, q)], axis=0)

    irel = lax.broadcasted_iota(jnp.int32, (tq, tq), 0)
    jrel = lax.broadcasted_iota(jnp.int32, (tq, tq), 1)
    dist0 = (irel - jrel).astype(F32)

    def step(kb, carry, diag):
        m, l, acc = carry
        start = pl.multiple_of(kb * tq, tq)
        k = k_ref[pl.ds(start, tq), :]
        v = v_ref[pl.ds(start, tq), :]
        s = lax.dot_general(qs, k, _NT, preferred_element_type=F32)
        off = ((qi - kb) * tq).astype(F32)
        bias = slope * (dist0 + off)
        s = s.reshape(2, tq, tq) - bias[None]
        if diag:
            s = jnp.where((dist0 >= 0.0)[None], s, NEG_INF)
        s = s.reshape(2 * tq, tq)
        m_new = jnp.maximum(m, jnp.max(s, axis=-1, keepdims=True))
        alpha = jnp.exp(m - m_new)
        p = jnp.exp(s - m_new)
        l = alpha * l + jnp.sum(p, axis=-1, keepdims=True)
        acc = alpha * acc + jnp.dot(p.astype(BF16), v, preferred_element_type=F32)
        return m_new, l, acc

    init = (jnp.full((2 * tq, 1), NEG_INF, F32), jnp.zeros((2 * tq, 1), F32),
            jnp.zeros((2 * tq, LANES), F32))
    carry = lax.fori_loop(0, qi, lambda kb, c: step(kb, c, False), init)
    _, l, acc = step(qi, carry, True)

    lp = lam_ref[...]
    lam = (jnp.exp(jnp.sum(lp[0:1] * lp[1:2], axis=-1, keepdims=True))
           - jnp.exp(jnp.sum(lp[2:3] * lp[3:4], axis=-1, keepdims=True)) + lam_init)
    o = acc / l
    a = o[:tq] - lam * o[tq:]
    ms = jnp.mean(a * a, axis=-1, keepdims=True)
    y = a * lax.rsqrt(ms + LN_EPS) * gain_ref[...] * (1.0 - lam_init)
    o_ref[...] = y.astype(o_ref.dtype)


def _diff_attention(proj, lam_params, gain, batch, seq, layer_idx, *, tq=256):
    t = proj.shape[0]
    tq = min(tq, seq)
    nq = seq // tq
    nh = DIFF_HEADS
    lam_init = 0.8 - 0.6 * math.exp(-0.3 * layer_idx)
    kern = functools.partial(_diff_attn_kernel, tq=tq, lam_init=lam_init)
    grid_spec = pltpu.PrefetchScalarGridSpec(
        num_scalar_prefetch=1,
        grid=(batch, nh, nq),
        in_specs=[pl.BlockSpec((tq, LANES), lambda b, h, qi, s: (b * nq + qi, h)),
                  pl.BlockSpec((seq, LANES), lambda b, h, qi, s: (b, nh + h)),
                  pl.BlockSpec((seq, LANES), lambda b, h, qi, s: (b, 2 * nh + h)),
                  pl.BlockSpec((4, DIFF_HEAD_DIM), lambda b, h, qi, s: (0, 0)),
                  pl.BlockSpec((1, LANES), lambda b, h, qi, s: (0, 0))],
        out_specs=pl.BlockSpec((tq, LANES), lambda b, h, qi, s: (b * nq + qi, h)),
    )
    return pl.pallas_call(
        kern,
        grid_spec=grid_spec,
        out_shape=jax.ShapeDtypeStruct((t, D_MODEL), BF16),
        compiler_params=_params(("parallel", "parallel", "arbitrary")),
    )(_slopes(nh), proj, proj, proj, lam_params.astype(F32), gain.astype(F32).reshape(1, LANES))


def _fox_c_kernel(h_ref, wf_ref, bf_ref, crow_ref, ccol_ref, carry_ref, *, tb):
    j = pl.program_id(1)

    @pl.when(j == 0)
    def _():
        carry_ref[...] = jnp.zeros_like(carry_ref)

    x = lax.dot_general(wf_ref[...].astype(BF16), h_ref[...], _NT, preferred_element_type=F32)
    x = x + bf_ref[...]
    lf = jnp.minimum(x, 0.0) - jnp.log1p(jnp.exp(-jnp.abs(x)))
    hi = lf.astype(BF16)
    r1 = lf - hi.astype(F32)
    mid = r1.astype(BF16)
    lo = (r1 - mid.astype(F32)).astype(BF16)
    ii = lax.broadcasted_iota(jnp.int32, (tb, tb), 0)
    jj = lax.broadcasted_iota(jnp.int32, (tb, tb), 1)
    tri = jnp.where(ii <= jj, 1.0, 0.0).astype(BF16)
    pre = (jnp.dot(hi, tri, preferred_element_type=F32)
           + jnp.dot(mid, tri, preferred_element_type=F32)
           + jnp.dot(lo, tri, preferred_element_type=F32))
    c = pre + carry_ref[...]
    crow_ref[0] = c
    ccol_ref[...] = c.T
    carry_ref[...] = c[:, tb - 1:tb]


def _fox_cumlog(hb, wf_t, bf_col, batch, seq, *, tb=512):
    t, d = hb.shape
    tb = min(tb, seq)
    nb = seq // tb
    kern = functools.partial(_fox_c_kernel, tb=tb)
    return pl.pallas_call(
        kern,
        grid=(batch, nb),
        in_specs=[pl.BlockSpec((tb, d), lambda b, j: (b * nb + j, 0)),
                  pl.BlockSpec((LANES, d), lambda b, j: (0, 0)),
                  pl.BlockSpec((LANES, 1), lambda b, j: (0, 0))],
        out_specs=[pl.BlockSpec((1, LANES, tb), lambda b, j: (b, 0, j)),
                   pl.BlockSpec((tb, LANES), lambda b, j: (b * nb + j, 0))],
        out_shape=[jax.ShapeDtypeStruct((batch, LANES, seq), F32),
                   jax.ShapeDtypeStruct((t, LANES), F32)],
        scratch_shapes=[pltpu.VMEM((LANES, 1), F32)],
        compiler_params=_params(("parallel", "arbitrary")),
    )(hb, wf_t, bf_col)


def _fox_attn_kernel(q_ref, k_ref, v_ref, crow_ref, ccol_ref, g_ref, o_ref, *, tq):
    h = pl.program_id(1)
    qi = pl.program_id(2)
    scale = FOX_HEAD_DIM ** -0.5
    q = q_ref[...]
    lane = lax.broadcasted_iota(jnp.int32, (tq, LANES), 1)
    ci = jnp.sum(jnp.where(lane == h, ccol_ref[...], 0.0), axis=1, keepdims=True)
    irel = lax.broadcasted_iota(jnp.int32, (tq, tq), 0)
    jrel = lax.broadcasted_iota(jnp.int32, (tq, tq), 1)
    causal = irel >= jrel

    def step(kb, carry, diag):
        m, l, acc = carry
        start = pl.multiple_of(kb * tq, tq)
        k = k_ref[pl.ds(start, tq), :]
        v = v_ref[pl.ds(start, tq), :]
        cj = crow_ref[0, pl.ds(kb, 1), :]
        s = lax.dot_general(q, k, _NT, preferred_element_type=F32) * scale
        s = s + (ci - cj)
        if diag:
            s = jnp.where(causal, s, NEG_INF)
        m_new = jnp.maximum(m, jnp.max(s, axis=-1, keepdims=True))
        alpha = jnp.exp(m - m_new)
        p = jnp.exp(s - m_new)
        l = alpha * l + jnp.sum(p, axis=-1, keepdims=True)
        acc = alpha * acc + jnp.dot(p.astype(BF16), v, preferred_element_type=F32)
        return m_new, l, acc

    init = (jnp.full((tq, 1), NEG_INF, F32), jnp.zeros((tq, 1), F32), jnp.zeros((tq, LANES), F32))
    carry = lax.fori_loop(0, qi, lambda kb, c: step(kb, c, False), init)
    _, l, acc = step(qi, carry, True)
    o_ref[...] = ((acc / l) * _sigmoid(g_ref[...])).astype(o_ref.dtype)


def _fox_attention(proj, crow, ccol, gate, batch, seq, *, tq=256):
    t = proj.shape[0]
    tq = min(tq, seq)
    nq = seq // tq
    nh = FOX_HEADS
    crow3 = crow.reshape(batch * LANES, nq, tq)
    kern = functools.partial(_fox_attn_kernel, tq=tq)
    return pl.pallas_call(
        kern,
        grid=(batch, nh, nq),
        in_specs=[pl.BlockSpec((tq, LANES), lambda b, h, qi: (b * nq + qi, h)),
                  pl.BlockSpec((seq, LANES), lambda b, h, qi: (b, nh + h)),
                  pl.BlockSpec((seq, LANES), lambda b, h, qi: (b, 2 * nh + h)),
                  pl.BlockSpec((1, nq, tq), lambda b, h, qi: (b * LANES + h, 0, 0)),
                  pl.BlockSpec((tq, LANES), lambda b, h, qi: (b * nq + qi, 0)),
                  pl.BlockSpec((tq, LANES), lambda b, h, qi: (b * nq + qi, h))],
        out_specs=pl.BlockSpec((tq, LANES), lambda b, h, qi: (b * nq + qi, h)),
        out_shape=jax.ShapeDtypeStruct((t, D_MODEL), BF16),
        compiler_params=_params(("parallel", "parallel", "arbitrary")),
    )(proj, proj, proj, crow3, ccol, gate)


def _swa_attn_kernel(sinks_ref, slopes_ref, q_ref, ko_ref, kp_ref, vo_ref, vp_ref, o_ref, *, tq):
    pair = pl.program_id(1)
    qi = pl.program_id(2)
    half = SWA_HEAD_DIM
    hk = pair // (SWA_Q_HEADS // SWA_KV_HEADS // 2)
    par = hk % 2
    tk = tq + SWA_BLOCK

    lane_k = lax.broadcasted_iota(jnp.int32, (tk, LANES), 1)
    keep = jnp.where(lane_k >= half, 1, 0) == par

    def dup(prev_ref, own_ref):
        x = jnp.concatenate([prev_ref[...], own_ref[...]], axis=0).astype(F32)
        return jnp.where(keep, x, pltpu.roll(x, half, 1)).astype(BF16)

    kd = dup(kp_ref, ko_ref)
    vd = dup(vp_ref, vo_ref)

    q = q_ref[...] * jnp.asarray(SWA_HEAD_DIM ** -0.5, BF16)
    lane = lax.broadcasted_iota(jnp.int32, (tq, LANES), 1)
    lo = lane < half
    zero = jnp.zeros_like(q)
    qs = jnp.concatenate([jnp.where(lo, q, zero), jnp.where(lo, zero, q)], axis=0)
    s = lax.dot_general(qs, kd, _NT, preferred_element_type=F32)

    r = lax.broadcasted_iota(jnp.int32, (tq, tk), 0)
    c = lax.broadcasted_iota(jnp.int32, (tq, tk), 1)
    dist = r - c + SWA_BLOCK
    valid = (dist >= 0) & (dist < SWA_BLOCK) & (qi * tq + r - dist >= 0)
    distf = dist.astype(F32)

    outs = []
    for hh in range(2):
        head = 2 * pair + hh
        sh = s[hh * tq:(hh + 1) * tq] - slopes_ref[head] * distf
        sh = jnp.where(valid, sh, NEG_INF)
        sink = sinks_ref[head]
        m = jnp.maximum(jnp.max(sh, axis=-1, keepdims=True), sink)
        p = jnp.exp(sh - m)
        l = jnp.sum(p, axis=-1, keepdims=True) + jnp.exp(sink - m)
        outs.append(jnp.dot(p.astype(BF16), vd, preferred_element_type=F32) / l)
    o_ref[...] = jnp.where(lo, outs[0], outs[1]).astype(o_ref.dtype)


def _swa_attention(proj, sinks, batch, seq, *, tq=256):
    t = proj.shape[0]
    tq = min(tq, seq)
    nq = seq // tq
    npairs = SWA_Q_HEADS // 2
    pairs_per_col = LANES // SWA_HEAD_DIM * (SWA_Q_HEADS // SWA_KV_HEADS) // 2
    kcol = SWA_Q_DIM // LANES
    vcol = (SWA_Q_DIM + SWA_KV_DIM) // LANES
    sub = tq // SWA_BLOCK

    def own(col):
        return lambda b, p, qi, *_: (b * nq + qi, col + p // pairs_per_col)

    def prev(col):
        return lambda b, p, qi, *_: (jnp.maximum((b * nq + qi) * sub - 1, 0), col + p // pairs_per_col)

    grid_spec = pltpu.PrefetchScalarGridSpec(
        num_scalar_prefetch=2,
        grid=(batch, npairs, nq),
        in_specs=[pl.BlockSpec((tq, LANES), lambda b, p, qi, *_: (b * nq + qi, p)),
                  pl.BlockSpec((tq, LANES), own(kcol)),
                  pl.BlockSpec((SWA_BLOCK, LANES), prev(kcol)),
                  pl.BlockSpec((tq, LANES), own(vcol)),
                  pl.BlockSpec((SWA_BLOCK, LANES), prev(vcol))],
        out_specs=pl.BlockSpec((tq, LANES), lambda b, p, qi, *_: (b * nq + qi, p)),
    )
    kern = functools.partial(_swa_attn_kernel, tq=tq)
    return pl.pallas_call(
        kern,
        grid_spec=grid_spec,
        out_shape=jax.ShapeDtypeStruct((t, SWA_Q_DIM), BF16),
        compiler_params=_params(("parallel", "parallel", "arbitrary")),
    )(sinks.astype(F32), _slopes(SWA_Q_HEADS), proj, proj, proj, proj, proj)


def _ffn_kernel(te_ref, nt_ref, x_ref, wg_ref, wu_ref, wd_ref, o_ref):
    t = pl.program_id(0)
    f = pl.program_id(1)
    live = t < nt_ref[0]

    @pl.when(live)
    def _():
        x = x_ref[...]
        g = jnp.dot(x, wg_ref[0].astype(BF16), preferred_element_type=F32)
        u = jnp.dot(x, wu_ref[0].astype(BF16), preferred_element_type=F32)
        a = (g * _sigmoid(g) * u).astype(BF16)
        d = jnp.dot(a, wd_ref[0].astype(BF16), preferred_element_type=F32)

        @pl.when(f == 0)
        def _():
            o_ref[...] = d

        @pl.when(f > 0)
        def _():
            o_ref[...] += d

    @pl.when(jnp.logical_not(live) & (f == 0))
    def _():
        o_ref[...] = jnp.zeros_like(o_ref)


def _ffn(x, w_gate, w_up, w_down, tile_expert, n_tiles, *, tm, tf=256):
    p, d = x.shape
    n_exp, _, ff = w_gate.shape
    nf = ff // tf
    nt_max = p // tm

    def xmap(t, f, te, nt):
        return (jnp.minimum(t, nt[0] - 1), 0)

    def fcol(t, f, nt):
        return jnp.where(t < nt[0], f, nf - 1)

    grid_spec = pltpu.PrefetchScalarGridSpec(
        num_scalar_prefetch=2,
        grid=(nt_max, nf),
        in_specs=[pl.BlockSpec((tm, d), xmap),
                  pl.BlockSpec((1, d, tf), lambda t, f, te, nt: (te[t], 0, fcol(t, f, nt))),
                  pl.BlockSpec((1, d, tf), lambda t, f, te, nt: (te[t], 0, fcol(t, f, nt))),
                  pl.BlockSpec((1, tf, d), lambda t, f, te, nt: (te[t], fcol(t, f, nt), 0))],
        out_specs=pl.BlockSpec((tm, d), lambda t, f, te, nt: (t, 0)),
    )
    return pl.pallas_call(
        _ffn_kernel,
        grid_spec=grid_spec,
        out_shape=jax.ShapeDtypeStruct((p, d), F32),
        compiler_params=_params(("arbitrary", "arbitrary")),
    )(tile_expert, n_tiles, x, w_gate, w_up, w_down)


def _router_kernel(h_ref, r_ref, o_ref):
    def split(x):
        hi = x.astype(BF16)
        return hi, (x - hi.astype(F32)).astype(BF16)

    h_hi, h_lo = split(h_ref[...])
    r_hi, r_lo = split(r_ref[...])
    logits = (jnp.dot(h_hi, r_hi, preferred_element_type=F32)
              + jnp.dot(h_hi, r_lo, preferred_element_type=F32)
              + jnp.dot(h_lo, r_hi, preferred_element_type=F32))
    lane = lax.broadcasted_iota(jnp.int32, logits.shape, 1)
    l1 = jnp.where(lane < N_EXPERTS, logits, NEG_INF)
    m1 = jnp.max(l1, axis=-1, keepdims=True)
    i1 = jnp.min(jnp.where(l1 == m1, lane, LANES), axis=-1, keepdims=True)
    l2 = jnp.where(lane == i1, NEG_INF, l1)
    m2 = jnp.max(l2, axis=-1, keepdims=True)
    i2 = jnp.min(jnp.where(l2 == m2, lane, LANES), axis=-1, keepdims=True)
    e = jnp.exp(m2 - m1)
    g1 = 1.0 / (1.0 + e)
    g2 = e / (1.0 + e)
    out = jnp.where(lane == 0, i1.astype(F32),
                    jnp.where(lane == 1, i2.astype(F32),
                              jnp.where(lane == 2, g1, jnp.where(lane == 3, g2, 0.0))))
    o_ref[...] = out


def _router(h, router_padded, *, tm=512):
    m, d = h.shape
    tm = min(tm, m)
    return pl.pallas_call(
        _router_kernel,
        grid=(m // tm,),
        in_specs=[pl.BlockSpec((tm, d), lambda i: (i, 0)),
                  pl.BlockSpec((d, LANES), lambda i: (0, 0))],
        out_specs=pl.BlockSpec((tm, LANES), lambda i: (i, 0)),
        out_shape=jax.ShapeDtypeStruct((m, LANES), F32),
        compiler_params=_params(("parallel",)),
    )(h, router_padded)


def _row_copy(src_hbm, row, dst, slot, sem):
    return pltpu.make_async_copy(src_hbm.at[pl.ds(row, 1), :], dst.at[pl.ds(slot, 1), :], sem)


def _gather_kernel(src_ref, h_hbm, o_ref, buf, sem, *, tg):
    base = pl.program_id(0) * tg

    def issue(r, carry):
        _row_copy(h_hbm, src_ref[base + r], buf, r, sem).start()
        return carry

    def drain(r, carry):
        _row_copy(h_hbm, 0, buf, r, sem).wait()
        return carry

    lax.fori_loop(0, tg, issue, 0)
    lax.fori_loop(0, tg, drain, 0)
    o_ref[...] = buf[...].astype(o_ref.dtype)


def _gather_rows(h, src, *, tg=512):
    p = src.shape[0]
    d = h.shape[1]
    kern = functools.partial(_gather_kernel, tg=tg)
    grid_spec = pltpu.PrefetchScalarGridSpec(
        num_scalar_prefetch=1,
        grid=(p // tg,),
        in_specs=[pl.BlockSpec(memory_space=pl.ANY)],
        out_specs=pl.BlockSpec((tg, d), lambda i, s: (i, 0)),
        scratch_shapes=[pltpu.VMEM((tg, d), F32), pltpu.SemaphoreType.DMA(())],
    )
    return pl.pallas_call(
        kern,
        grid_spec=grid_spec,
        out_shape=jax.ShapeDtypeStruct((p, d), BF16),
        compiler_params=_params(("arbitrary",)),
    )(src, h)


def _combine_ln_kernel(pos_ref, y_hbm, route_ref, h_ref, g_ref, b_ref, hf_ref, hb_ref,
                       buf0, buf1, sem, *, tm):
    base = pl.program_id(0) * tm

    def issue(r, carry):
        _row_copy(y_hbm, pos_ref[2 * (base + r)], buf0, r, sem).start()
        _row_copy(y_hbm, pos_ref[2 * (base + r) + 1], buf1, r, sem).start()
        return carry

    def drain(r, carry):
        _row_copy(y_hbm, 0, buf0, r, sem).wait()
        _row_copy(y_hbm, 0, buf1, r, sem).wait()
        return carry

    lax.fori_loop(0, tm, issue, 0)
    lax.fori_loop(0, tm, drain, 0)
    route = route_ref[...]
    y = route[:, 2:3] * buf0[...] + route[:, 3:4] * buf1[...]
    out = _layer_norm_rows(ALPHA * h_ref[...] + y, g_ref[...], b_ref[...])
    hf_ref[...] = out
    hb_ref[...] = out.astype(BF16)


def _combine_ln(pos, ys, route, h, g, b, *, tm=256):
    m, d = h.shape
    tm = min(tm, m)
    kern = functools.partial(_combine_ln_kernel, tm=tm)
    row = pl.BlockSpec((tm, d), lambda i, s: (i, 0))
    vec = pl.BlockSpec((1, d), lambda i, s: (0, 0))
    grid_spec = pltpu.PrefetchScalarGridSpec(
        num_scalar_prefetch=1,
        grid=(m // tm,),
        in_specs=[pl.BlockSpec(memory_space=pl.ANY),
                  pl.BlockSpec((tm, LANES), lambda i, s: (i, 0)),
                  row, vec, vec],
        out_specs=[row, row],
        scratch_shapes=[pltpu.VMEM((tm, d), F32), pltpu.VMEM((tm, d), F32),
                        pltpu.SemaphoreType.DMA(())],
    )
    return pl.pallas_call(
        kern,
        grid_spec=grid_spec,
        out_shape=[jax.ShapeDtypeStruct((m, d), F32), jax.ShapeDtypeStruct((m, d), BF16)],
        compiler_params=_params(("arbitrary",)),
    )(pos, ys, route, h, g, b)


def _moe_layer(hf, hb, router, w_gate, w_up, w_down, g, b, *, tm):
    t, d = hf.shape
    tm = min(tm, t)
    route = _router(hf, jnp.pad(router.astype(F32), ((0, 0), (0, LANES - N_EXPERTS))))

    expert = route[:, :TOP_K].astype(jnp.int32).reshape(-1)
    onehot = (expert[:, None] == jnp.arange(N_EXPERTS, dtype=jnp.int32)[None]).astype(jnp.int32)
    csum = jnp.cumsum(onehot, axis=0)
    counts = csum[-1]
    rank = jnp.take_along_axis(csum, expert[:, None], axis=1)[:, 0] - 1
    tiles_per = (counts + tm - 1) // tm
    tile_end = jnp.cumsum(tiles_per)
    start = (tile_end - tiles_per) * tm
    pos = (start[expert] + rank).astype(jnp.int32)
    nt_max = (t * TOP_K) // tm + N_EXPERTS
    n_tiles = tile_end[-1].astype(jnp.int32)
    src = jnp.zeros((nt_max * tm,), jnp.int32).at[pos].set(
        jnp.arange(t * TOP_K, dtype=jnp.int32) // TOP_K)
    tidx = jnp.arange(nt_max, dtype=jnp.int32)
    tile_expert = jnp.searchsorted(tile_end, jnp.minimum(tidx, n_tiles - 1), side="right")
    tile_expert = jnp.minimum(tile_expert, N_EXPERTS - 1).astype(jnp.int32)

    xs = _gather_rows(hf, src)
    ys = _ffn(xs, w_gate, w_up, w_down, tile_expert, n_tiles.reshape(1), tm=tm)
    return _combine_ln(pos, ys, route, hf, g, b)


def _dense_layer(hf, hb, w_gate, w_up, w_down, g, b, *, tm):
    t = hf.shape[0]
    tm = min(tm, t)
    nt = t // tm
    ys = _ffn(hb, w_gate[None], w_up[None], w_down[None],
              jnp.zeros((nt,), jnp.int32), jnp.full((1,), nt, jnp.int32), tm=tm)
    return _res_ln(ys, hf, g, b)


def _to_bf16_kernel(x_ref, o_ref):
    o_ref[...] = x_ref[...].astype(BF16)


def _to_bf16(x, *, tm=512):
    m, n = x.shape
    tm = min(tm, m)
    spec = pl.BlockSpec((tm, n), lambda i: (i, 0))
    return pl.pallas_call(
        _to_bf16_kernel, grid=(m // tm,), in_specs=[spec], out_specs=spec,
        out_shape=jax.ShapeDtypeStruct((m, n), BF16),
        compiler_params=_params(("parallel",)),
    )(x)


def _row(v):
    return v.astype(F32).reshape(1, -1)


def kernel(x, ln_gain, ln_bias, diff_w_in, diff_lambda, diff_subln_gain, diff_w_out,
           fox_w_in, fox_b_f, fox_qk_gain, fox_w_out, swa_w_in, swa_sinks, swa_w_out,
           dense_w_gate, dense_w_up, dense_w_down, moe_router, moe_w_gate, moe_w_up,
           moe_w_down, *, ffn_tm=1024):
    batch, seq, d = x.shape
    t = batch * seq
    hf = x.reshape(t, d).astype(F32)
    hb = _to_bf16(hf)
    depth = ln_gain.shape[0]
    for i in range(depth):
        kind, j = i % 3, i // 3
        if kind == 0:
            proj = _proj(hb, diff_w_in[j], 3 * D_MODEL)
            o = _diff_attention(proj, diff_lambda[j], diff_subln_gain[j], batch, seq, i)
            w_out = diff_w_out[j]
        elif kind == 1:
            w_in = fox_w_in[j]
            qkv = 3 * D_MODEL
            gain = jnp.concatenate([jnp.tile(fox_qk_gain[j, 0].astype(F32), FOX_HEADS),
                                    jnp.tile(fox_qk_gain[j, 1].astype(F32), FOX_HEADS),
                                    jnp.ones((D_MODEL,), F32)]).reshape(1, qkv)
            proj = _proj(hb, w_in, qkv, gain=gain, n_norm=2 * D_MODEL)
            gate = _proj(hb, w_in[:, qkv + FOX_HEADS:], D_MODEL, out_dtype=F32)
            wf_t = jnp.pad(w_in[:, qkv:qkv + FOX_HEADS].T, ((0, LANES - FOX_HEADS), (0, 0)))
            bf_col = jnp.pad(fox_b_f[j].astype(F32), (0, LANES - FOX_HEADS)).reshape(LANES, 1)
            crow, ccol = _fox_cumlog(hb, wf_t, bf_col, batch, seq)
            o = _fox_attention(proj, crow, ccol, gate, batch, seq)
            w_out = fox_w_out[j]
        else:
            proj = _proj(hb, swa_w_in[j], SWA_Q_DIM + 2 * SWA_KV_DIM)
            o = _swa_attention(proj, swa_sinks[j], batch, seq)
            w_out = swa_w_out[j]
        hf, hb = _outproj_ln(o, w_out, hf, _row(ln_gain[i, 0]), _row(ln_bias[i, 0]))
        c = i // 2
        g, b = _row(ln_gain[i, 1]), _row(ln_bias[i, 1])
        if i % 2 == 0:
            hf, hb = _dense_layer(hf, hb, dense_w_gate[c], dense_w_up[c], dense_w_down[c], g, b,
                                  tm=ffn_tm)
        else:
            hf, hb = _moe_layer(hf, hb, moe_router[c], moe_w_gate[c], moe_w_up[c], moe_w_down[c],
                                g, b, tm=ffn_tm)
    return hf.reshape(batch, seq, d)
```

```python
import functools
import math

import numpy as np
import jax
import jax.numpy as jnp
from jax import lax
from jax.experimental import pallas as pl
from jax.experimental.pallas import tpu as pltpu

F32 = jnp.float32
BF16 = jnp.bfloat16

D_MODEL = 2048
DEPTH = 4
LN_EPS = 1e-5
ALPHA = (2 * DEPTH) ** 0.25

DIFF_HEADS = 16
DIFF_HEAD_DIM = 64
FOX_HEADS = 16
FOX_HEAD_DIM = 128
SWA_Q_HEADS = 32
SWA_KV_HEADS = 4
SWA_HEAD_DIM = 64
SWA_BLOCK = 128
SWA_Q_DIM = SWA_Q_HEADS * SWA_HEAD_DIM
SWA_KV_DIM = SWA_KV_HEADS * SWA_HEAD_DIM
N_EXPERTS = 8
TOP_K = 2

LANES = 128
VMEM_LIMIT = 56 * 1024 * 1024

NEG_INF = float("-inf")
_NT = (((1,), (1,)), ((), ()))


def _slopes(n):
    def pow2(m):
        start = 2.0 ** (-8.0 / m)
        return [start ** (i + 1) for i in range(m)]
    if n & (n - 1) == 0:
        s = pow2(n)
    else:
        c = 2 ** int(math.floor(math.log2(n)))
        s = pow2(c) + pow2(2 * c)[0::2][: n - c]
    return jnp.asarray(np.array(s, dtype=np.float32))


def _params(sem, vmem=VMEM_LIMIT):
    return pltpu.CompilerParams(dimension_semantics=sem, vmem_limit_bytes=vmem)


def _layer_norm_rows(y, g, b):
    mu = jnp.mean(y, axis=-1, keepdims=True)
    yc = y - mu
    var = jnp.mean(yc * yc, axis=-1, keepdims=True)
    return yc * lax.rsqrt(var + LN_EPS) * g + b


def _sigmoid(x):
    return 1.0 / (1.0 + jnp.exp(-x))


def _proj_kernel(a_ref, w_ref, g_ref, o_ref, *, n_norm_tiles, tn):
    acc = jnp.dot(a_ref[...], w_ref[...].astype(BF16), preferred_element_type=F32)
    if n_norm_tiles == 0:
        o_ref[...] = acc.astype(o_ref.dtype)
        return
    j = pl.program_id(1)

    @pl.when(j < n_norm_tiles)
    def _():
        for c in range(tn // LANES):
            sl = slice(c * LANES, (c + 1) * LANES)
            blk = acc[:, sl]
            ms = jnp.mean(blk * blk, axis=-1, keepdims=True)
            o_ref[:, sl] = (blk * lax.rsqrt(ms + LN_EPS) * g_ref[:, sl]).astype(o_ref.dtype)

    @pl.when(j >= n_norm_tiles)
    def _():
        o_ref[...] = acc.astype(o_ref.dtype)


def _proj(a, w, layer, n_out, *, gain=None, n_norm=0, out_dtype=BF16, tm=1024, tn=512):
    m, k = a.shape
    tm = min(tm, m)
    tn = min(tn, n_out)
    if gain is None:
        gain = jnp.ones((1, n_out), F32)
    kern = functools.partial(_proj_kernel, n_norm_tiles=n_norm // tn, tn=tn)
    return pl.pallas_call(
        kern,
        grid=(m // tm, n_out // tn),
        in_specs=[pl.BlockSpec((tm, k), lambda i, j: (i, 0)),
                  pl.BlockSpec((None, k, tn), lambda i, j: (layer, 0, j)),
                  pl.BlockSpec((1, tn), lambda i, j: (0, j))],
        out_specs=pl.BlockSpec((tm, tn), lambda i, j: (i, j)),
        out_shape=jax.ShapeDtypeStruct((m, n_out), out_dtype),
        compiler_params=_params(("parallel", "arbitrary")),
    )(a, w, gain)


def _outproj_ln_kernel(o_ref, w_ref, h_ref, g_ref, b_ref, hf_ref, hb_ref, acc_ref):
    k = pl.program_id(1)

    @pl.when(k == 0)
    def _():
        acc_ref[...] = jnp.zeros_like(acc_ref)

    acc_ref[...] += jnp.dot(o_ref[...], w_ref[...].astype(BF16), preferred_element_type=F32)

    @pl.when(k == pl.num_programs(1) - 1)
    def _():
        out = _layer_norm_rows(ALPHA * h_ref[...] + acc_ref[...], g_ref[...], b_ref[...])
        hf_ref[...] = out
        hb_ref[...] = out.astype(BF16)


def _outproj_ln(o, w, layer, h, g, b, *, tm=512, tk=512):
    m, k = o.shape
    n = w.shape[2]
    tm = min(tm, m)
    return pl.pallas_call(
        _outproj_ln_kernel,
        grid=(m // tm, k // tk),
        in_specs=[pl.BlockSpec((tm, tk), lambda i, kk: (i, kk)),
                  pl.BlockSpec((None, tk, n), lambda i, kk: (layer, kk, 0)),
                  pl.BlockSpec((tm, n), lambda i, kk: (i, 0)),
                  pl.BlockSpec((1, n), lambda i, kk: (0, 0)),
                  pl.BlockSpec((1, n), lambda i, kk: (0, 0))],
        out_specs=[pl.BlockSpec((tm, n), lambda i, kk: (i, 0)),
                   pl.BlockSpec((tm, n), lambda i, kk: (i, 0))],
        out_shape=[jax.ShapeDtypeStruct((m, n), F32), jax.ShapeDtypeStruct((m, n), BF16)],
        scratch_shapes=[pltpu.VMEM((tm, n), F32)],
        compiler_params=_params(("parallel", "arbitrary")),
    )(o, w, h, g, b)


def _res_ln_kernel(y_ref, h_ref, g_ref, b_ref, hf_ref, hb_ref):
    out = _layer_norm_rows(ALPHA * h_ref[...] + y_ref[...], g_ref[...], b_ref[...])
    hf_ref[...] = out
    hb_ref[...] = out.astype(BF16)


def _res_ln(y, h, g, b, *, tm=512):
    m, n = h.shape
    tm = min(tm, m)
    row = pl.BlockSpec((tm, n), lambda i: (i, 0))
    vec = pl.BlockSpec((1, n), lambda i: (0, 0))
    return pl.pallas_call(
        _res_ln_kernel,
        grid=(m // tm,),
        in_specs=[row, row, vec, vec],
        out_specs=[row, row],
        out_shape=[jax.ShapeDtypeStruct((m, n), F32), jax.ShapeDtypeStruct((m, n), BF16)],
        compiler_params=_params(("parallel",)),
    )(y, h, g, b)


ATTN_TILE = 512


def _split3(x):
    hi = x.astype(BF16).astype(F32)
    r = x - hi
    mid = r.astype(BF16).astype(F32)
    lo = (r - mid).astype(BF16).astype(F32)
    return hi, mid, lo


def _lane_groups(lane, groups):
    out = jnp.zeros(lane.shape, F32)
    for g, parts in enumerate(groups):
        for i, part in enumerate(parts):
            out = jnp.where(lane == 3 * g + i, part, out)
    return out


def _diff_attn_kernel(slopes_ref, q_ref, k_ref, v_ref, lam_ref, gain_ref, o_ref, *, tq, lam_init):
    h = pl.program_id(1)
    qi = pl.program_id(2)
    half = DIFF_HEAD_DIM

    lane = lax.broadcasted_iota(jnp.int32, (tq, LANES), 1)
    row = lax.broadcasted_iota(jnp.int32, (tq, LANES), 0)
    slope = jnp.full((tq, LANES), slopes_ref[h], F32)
    sl = _split3(slope)
    neg_rows = _split3(-(slope * row.astype(F32)))
    q_bias = _lane_groups(lane, [sl, tuple(256.0 * x for x in sl), tuple(-x for x in sl), neg_rows])
    ones = (1.0, 1.0, 1.0)
    j_lo = (row & 255).astype(F32)
    j_hi = (row >> 8).astype(F32)
    k_bias_const = _lane_groups(lane, [(j_lo,) * 3, (j_hi,) * 3, (0.0,) * 3, ones])
    off_lanes = (lane >= 6) & (lane < 9)

    q = q_ref[...] * jnp.asarray(DIFF_HEAD_DIM ** -0.5, BF16)
    zero = jnp.zeros_like(q)
    qb = q_bias.astype(BF16)
    qs = jnp.concatenate(
        [jnp.concatenate([jnp.where(lane < half, q, zero), qb], axis=1),
         jnp.concatenate([jnp.where(lane < half, zero, q), qb], axis=1)], axis=0)

    irel = lax.broadcasted_iota(jnp.int32, (tq, tq), 0)
    jrel = lax.broadcasted_iota(jnp.int32, (tq, tq), 1)
    causal = jnp.concatenate([irel >= jrel, irel >= jrel], axis=0)

    def step(kb, carry, diag):
        m, l, acc = carry
        start = pl.multiple_of(kb * tq, tq)
        off = ((qi - kb) * tq).astype(F32)
        kb_bias = jnp.where(off_lanes, off, k_bias_const).astype(BF16)
        k = jnp.concatenate([k_ref[pl.ds(start, tq), :], kb_bias], axis=1)
        v = v_ref[pl.ds(start, tq), :]
        s = lax.dot_general(qs, k, _NT, preferred_element_type=F32)
        if diag:
            s = jnp.where(causal, s, NEG_INF)
        m_new = jnp.maximum(m, jnp.max(s, axis=-1, keepdims=True))
        alpha = jnp.exp(m - m_new)
        p = jnp.exp(s - m_new)
        l = alpha * l + jnp.sum(p, axis=-1, keepdims=True)
        acc = alpha * acc + jnp.dot(p.astype(BF16), v, preferred_element_type=F32)
        return m_new, l, acc

    init = (jnp.full((2 * tq, 1), NEG_INF, F32), jnp.zeros((2 * tq, 1), F32),
            jnp.zeros((2 * tq, LANES), F32))
    carry = lax.fori_loop(0, qi, lambda kb, c: step(kb, c, False), init)
    _, l, acc = step(qi, carry, True)

    lp = lam_ref[...]
    lam = (jnp.exp(jnp.sum(lp[0:1] * lp[1:2], axis=-1, keepdims=True))
           - jnp.exp(jnp.sum(lp[2:3] * lp[3:4], axis=-1, keepdims=True)) + lam_init)
    o = acc / l
    a = o[:tq] - lam * o[tq:]
    ms = jnp.mean(a * a, axis=-1, keepdims=True)
    y = a * lax.rsqrt(ms + LN_EPS) * gain_ref[...] * (1.0 - lam_init)
    o_ref[...] = y.astype(o_ref.dtype)


def _diff_attention(proj, lam_params, gain, batch, seq, layer_idx, *, tq=ATTN_TILE):
    t = proj.shape[0]
    tq = min(tq, seq)
    nq = seq // tq
    nh = DIFF_HEADS
    lam_init = 0.8 - 0.6 * math.exp(-0.3 * layer_idx)
    kern = functools.partial(_diff_attn_kernel, tq=tq, lam_init=lam_init)
    grid_spec = pltpu.PrefetchScalarGridSpec(
        num_scalar_prefetch=1,
        grid=(batch, nh, nq),
        in_specs=[pl.BlockSpec((tq, LANES), lambda b, h, qi, s: (b * nq + qi, h)),
                  pl.BlockSpec((seq, LANES), lambda b, h, qi, s: (b, nh + h)),
                  pl.BlockSpec((seq, LANES), lambda b, h, qi, s: (b, 2 * nh + h)),
                  pl.BlockSpec((4, DIFF_HEAD_DIM), lambda b, h, qi, s: (0, 0)),
                  pl.BlockSpec((1, LANES), lambda b, h, qi, s: (0, 0))],
        out_specs=pl.BlockSpec((tq, LANES), lambda b, h, qi, s: (b * nq + qi, h)),
    )
    return pl.pallas_call(
        kern,
        grid_spec=grid_spec,
        out_shape=jax.ShapeDtypeStruct((t, D_MODEL), BF16),
        compiler_params=_params(("parallel", "parallel", "arbitrary")),
    )(_slopes(nh), proj, proj, proj, lam_params.astype(F32), gain.astype(F32).reshape(1, LANES))


def _fox_c_kernel(h_ref, wf_ref, bf_ref, ccol_ref, carry_ref, *, tb):
    j = pl.program_id(1)

    @pl.when(j == 0)
    def _():
        carry_ref[...] = jnp.zeros_like(carry_ref)

    x = lax.dot_general(wf_ref[...].astype(BF16), h_ref[...], _NT, preferred_element_type=F32)
    x = x + bf_ref[...]
    lf = jnp.minimum(x, 0.0) - jnp.log1p(jnp.exp(-jnp.abs(x)))
    ii = lax.broadcasted_iota(jnp.int32, (tb, tb), 0)
    jj = lax.broadcasted_iota(jnp.int32, (tb, tb), 1)
    tri = jnp.where(ii <= jj, 1.0, 0.0).astype(BF16)
    pre = sum(jnp.dot(part.astype(BF16), tri, preferred_element_type=F32) for part in _split3(lf))
    c = pre + carry_ref[...]
    ccol_ref[...] = c.T
    carry_ref[...] = c[:, tb - 1:tb]


def _fox_cumlog(hb, wf_t, bf_col, batch, seq, *, tb=512):
    t, d = hb.shape
    tb = min(tb, seq)
    nb = seq // tb
    kern = functools.partial(_fox_c_kernel, tb=tb)
    return pl.pallas_call(
        kern,
        grid=(batch, nb),
        in_specs=[pl.BlockSpec((tb, d), lambda b, j: (b * nb + j, 0)),
                  pl.BlockSpec((LANES, d), lambda b, j: (0, 0)),
                  pl.BlockSpec((LANES, 1), lambda b, j: (0, 0))],
        out_specs=pl.BlockSpec((tb, LANES), lambda b, j: (b * nb + j, 0)),
        out_shape=jax.ShapeDtypeStruct((t, LANES), F32),
        scratch_shapes=[pltpu.VMEM((LANES, 1), F32)],
        compiler_params=_params(("parallel", "arbitrary")),
    )(hb, wf_t, bf_col)


def _fox_attn_kernel(q_ref, k_ref, v_ref, cq_ref, ck_ref, g_ref, o_ref, *, tq):
    h = pl.program_id(1)
    qi = pl.program_id(2)
    scale = FOX_HEAD_DIM ** -0.5
    c2 = scale * math.log2(math.e)

    lane = lax.broadcasted_iota(jnp.int32, (tq, LANES), 1)
    head = lane == h

    def head_col(x):
        return jnp.sum(jnp.where(head, x, 0.0), axis=1, keepdims=True)

    ci = head_col(cq_ref[...])
    cref = ci[0:1, :]
    ones = (1.0, 1.0, 1.0)
    q_bias = _lane_groups(lane, [_split3(jnp.broadcast_to((ci - cref) * (1.0 / scale), (tq, LANES))), ones])
    qa = jnp.concatenate([q_ref[...], q_bias.astype(BF16)], axis=1)

    irel = lax.broadcasted_iota(jnp.int32, (tq, tq), 0)
    jrel = lax.broadcasted_iota(jnp.int32, (tq, tq), 1)
    causal = irel >= jrel

    def step(kb, carry, diag):
        m, l, acc = carry
        start = pl.multiple_of(kb * tq, tq)
        cj = head_col(ck_ref[pl.ds(start, tq), :])
        neg = jnp.broadcast_to((cref - cj) * (1.0 / scale), (tq, LANES))
        k_bias = _lane_groups(lane, [ones, _split3(neg)])
        k = jnp.concatenate([k_ref[pl.ds(start, tq), :], k_bias.astype(BF16)], axis=1)
        v = v_ref[pl.ds(start, tq), :]
        s = lax.dot_general(qa, k, _NT, preferred_element_type=F32)
        if diag:
            s = jnp.where(causal, s, NEG_INF)
        m_new = jnp.maximum(m, jnp.max(s, axis=-1, keepdims=True))
        alpha = jnp.exp2((m - m_new) * c2)
        p = jnp.exp2((s - m_new) * c2)
        l = alpha * l + jnp.sum(p, axis=-1, keepdims=True)
        acc = alpha * acc + jnp.dot(p.astype(BF16), v, preferred_element_type=F32)
        return m_new, l, acc

    init = (jnp.full((tq, 1), NEG_INF, F32), jnp.zeros((tq, 1), F32), jnp.zeros((tq, LANES), F32))
    carry = lax.fori_loop(0, qi, lambda kb, c: step(kb, c, False), init)
    _, l, acc = step(qi, carry, True)
    o_ref[...] = ((acc / l) * _sigmoid(g_ref[...])).astype(o_ref.dtype)


def _fox_attention(proj, ccol, gate, batch, seq, *, tq=ATTN_TILE):
    t = proj.shape[0]
    tq = min(tq, seq)
    nq = seq // tq
    nh = FOX_HEADS
    kern = functools.partial(_fox_attn_kernel, tq=tq)
    return pl.pallas_call(
        kern,
        grid=(batch, nh, nq),
        in_specs=[pl.BlockSpec((tq, LANES), lambda b, h, qi: (b * nq + qi, h)),
                  pl.BlockSpec((seq, LANES), lambda b, h, qi: (b, nh + h)),
                  pl.BlockSpec((seq, LANES), lambda b, h, qi: (b, 2 * nh + h)),
                  pl.BlockSpec((tq, LANES), lambda b, h, qi: (b * nq + qi, 0)),
                  pl.BlockSpec((seq, LANES), lambda b, h, qi: (b, 0)),
                  pl.BlockSpec((tq, LANES), lambda b, h, qi: (b * nq + qi, h))],
        out_specs=pl.BlockSpec((tq, LANES), lambda b, h, qi: (b * nq + qi, h)),
        out_shape=jax.ShapeDtypeStruct((t, D_MODEL), BF16),
        compiler_params=_params(("parallel", "parallel", "arbitrary")),
    )(proj, proj, proj, ccol, ccol, gate)


def _swa_attn_kernel(sinks_ref, slopes_ref, q_ref, ko_ref, kp_ref, vo_ref, vp_ref, o_ref, *, tq):
    pair = pl.program_id(1)
    qi = pl.program_id(2)
    half = SWA_HEAD_DIM
    hk = pair // (SWA_Q_HEADS // SWA_KV_HEADS // 2)
    par = hk % 2
    tk = tq + SWA_BLOCK

    lane_k = lax.broadcasted_iota(jnp.int32, (tk, LANES), 1)
    keep = jnp.where(lane_k >= half, 1, 0) == par

    def dup(prev_ref, own_ref):
        x = jnp.concatenate([prev_ref[...], own_ref[...]], axis=0).astype(F32)
        return jnp.where(keep, x, pltpu.roll(x, half, 1)).astype(BF16)

    kd = dup(kp_ref, ko_ref)
    vd = dup(vp_ref, vo_ref)

    q = q_ref[...] * jnp.asarray(SWA_HEAD_DIM ** -0.5, BF16)
    lane = lax.broadcasted_iota(jnp.int32, (tq, LANES), 1)
    lo = lane < half
    zero = jnp.zeros_like(q)
    qs = jnp.concatenate([jnp.where(lo, q, zero), jnp.where(lo, zero, q)], axis=0)
    s = lax.dot_general(qs, kd, _NT, preferred_element_type=F32)

    r = lax.broadcasted_iota(jnp.int32, (tq, tk), 0)
    c = lax.broadcasted_iota(jnp.int32, (tq, tk), 1)
    dist = r - c + SWA_BLOCK
    valid = (dist >= 0) & (dist < SWA_BLOCK) & (qi * tq + r - dist >= 0)
    distf = dist.astype(F32)

    outs = []
    for hh in range(2):
        head = 2 * pair + hh
        sh = s[hh * tq:(hh + 1) * tq] - slopes_ref[head] * distf
        sh = jnp.where(valid, sh, NEG_INF)
        sink = sinks_ref[head]
        m = jnp.maximum(jnp.max(sh, axis=-1, keepdims=True), sink)
        p = jnp.exp(sh - m)
        l = jnp.sum(p, axis=-1, keepdims=True) + jnp.exp(sink - m)
        outs.append(jnp.dot(p.astype(BF16), vd, preferred_element_type=F32) / l)
    o_ref[...] = jnp.where(lo, outs[0], outs[1]).astype(o_ref.dtype)


def _swa_attention(proj, sinks, batch, seq, *, tq=256):
    t = proj.shape[0]
    tq = min(tq, seq)
    nq = seq // tq
    npairs = SWA_Q_HEADS // 2
    pairs_per_col = LANES // SWA_HEAD_DIM * (SWA_Q_HEADS // SWA_KV_HEADS) // 2
    kcol = SWA_Q_DIM // LANES
    vcol = (SWA_Q_DIM + SWA_KV_DIM) // LANES
    sub = tq // SWA_BLOCK

    def own(col):
        return lambda b, p, qi, *_: (b * nq + qi, col + p // pairs_per_col)

    def prev(col):
        return lambda b, p, qi, *_: (jnp.maximum((b * nq + qi) * sub - 1, 0), col + p // pairs_per_col)

    grid_spec = pltpu.PrefetchScalarGridSpec(
        num_scalar_prefetch=2,
        grid=(batch, npairs, nq),
        in_specs=[pl.BlockSpec((tq, LANES), lambda b, p, qi, *_: (b * nq + qi, p)),
                  pl.BlockSpec((tq, LANES), own(kcol)),
                  pl.BlockSpec((SWA_BLOCK, LANES), prev(kcol)),
                  pl.BlockSpec((tq, LANES), own(vcol)),
                  pl.BlockSpec((SWA_BLOCK, LANES), prev(vcol))],
        out_specs=pl.BlockSpec((tq, LANES), lambda b, p, qi, *_: (b * nq + qi, p)),
    )
    kern = functools.partial(_swa_attn_kernel, tq=tq)
    return pl.pallas_call(
        kern,
        grid_spec=grid_spec,
        out_shape=jax.ShapeDtypeStruct((t, SWA_Q_DIM), BF16),
        compiler_params=_params(("parallel", "parallel", "arbitrary")),
    )(sinks.astype(F32), _slopes(SWA_Q_HEADS), proj, proj, proj, proj, proj)


def _ffn_kernel(te_ref, nt_ref, x_ref, wg_ref, wu_ref, wd_ref, o_ref):
    t = pl.program_id(0)
    f = pl.program_id(1)
    live = t < nt_ref[0]

    @pl.when(live)
    def _():
        x = x_ref[...]
        g = jnp.dot(x, wg_ref[0].astype(BF16), preferred_element_type=F32)
        u = jnp.dot(x, wu_ref[0].astype(BF16), preferred_element_type=F32)
        a = (g * _sigmoid(g) * u).astype(BF16)
        d = jnp.dot(a, wd_ref[0].astype(BF16), preferred_element_type=F32)

        @pl.when(f == 0)
        def _():
            o_ref[...] = d

        @pl.when(f > 0)
        def _():
            o_ref[...] += d

    @pl.when(jnp.logical_not(live) & (f == 0))
    def _():
        o_ref[...] = jnp.zeros_like(o_ref)


def _ffn(x, w_gate, w_up, w_down, tile_expert, n_tiles, *, tm, tf=256):
    p, d = x.shape
    n_exp, _, ff = w_gate.shape
    nf = ff // tf
    nt_max = p // tm

    def xmap(t, f, te, nt):
        return (jnp.minimum(t, nt[0] - 1), 0)

    def fcol(t, f, nt):
        return jnp.where(t < nt[0], f, nf - 1)

    grid_spec = pltpu.PrefetchScalarGridSpec(
        num_scalar_prefetch=2,
        grid=(nt_max, nf),
        in_specs=[pl.BlockSpec((tm, d), xmap),
                  pl.BlockSpec((1, d, tf), lambda t, f, te, nt: (te[t], 0, fcol(t, f, nt))),
                  pl.BlockSpec((1, d, tf), lambda t, f, te, nt: (te[t], 0, fcol(t, f, nt))),
                  pl.BlockSpec((1, tf, d), lambda t, f, te, nt: (te[t], fcol(t, f, nt), 0))],
        out_specs=pl.BlockSpec((tm, d), lambda t, f, te, nt: (t, 0)),
    )
    return pl.pallas_call(
        _ffn_kernel,
        grid_spec=grid_spec,
        out_shape=jax.ShapeDtypeStruct((p, d), F32),
        compiler_params=_params(("arbitrary", "arbitrary")),
    )(tile_expert, n_tiles, x, w_gate, w_up, w_down)


def _router_kernel(h_ref, r_ref, o_ref):
    def split(x):
        hi = x.astype(BF16)
        return hi, (x - hi.astype(F32)).astype(BF16)

    h_hi, h_lo = split(h_ref[...])
    r_hi, r_lo = split(r_ref[...])
    logits = (jnp.dot(h_hi, r_hi, preferred_element_type=F32)
              + jnp.dot(h_hi, r_lo, preferred_element_type=F32)
              + jnp.dot(h_lo, r_hi, preferred_element_type=F32))
    lane = lax.broadcasted_iota(jnp.int32, logits.shape, 1)
    l1 = jnp.where(lane < N_EXPERTS, logits, NEG_INF)
    m1 = jnp.max(l1, axis=-1, keepdims=True)
    i1 = jnp.min(jnp.where(l1 == m1, lane, LANES), axis=-1, keepdims=True)
    l2 = jnp.where(lane == i1, NEG_INF, l1)
    m2 = jnp.max(l2, axis=-1, keepdims=True)
    i2 = jnp.min(jnp.where(l2 == m2, lane, LANES), axis=-1, keepdims=True)
    e = jnp.exp(m2 - m1)
    g1 = 1.0 / (1.0 + e)
    g2 = e / (1.0 + e)
    out = jnp.where(lane == 0, i1.astype(F32),
                    jnp.where(lane == 1, i2.astype(F32),
                              jnp.where(lane == 2, g1, jnp.where(lane == 3, g2, 0.0))))
    o_ref[...] = out


def _router(h, router_padded, *, tm=512):
    m, d = h.shape
    tm = min(tm, m)
    return pl.pallas_call(
        _router_kernel,
        grid=(m // tm,),
        in_specs=[pl.BlockSpec((tm, d), lambda i: (i, 0)),
                  pl.BlockSpec((d, LANES), lambda i: (0, 0))],
        out_specs=pl.BlockSpec((tm, LANES), lambda i: (i, 0)),
        out_shape=jax.ShapeDtypeStruct((m, LANES), F32),
        compiler_params=_params(("parallel",)),
    )(h, router_padded)


def _row_copy(src_hbm, row, dst, slot, sem):
    return pltpu.make_async_copy(src_hbm.at[pl.ds(row, 1), :], dst.at[pl.ds(slot, 1), :], sem)


def _gather_kernel(src_ref, h_hbm, o_ref, buf, sem, *, tg):
    base = pl.program_id(0) * tg

    def issue(r, carry):
        _row_copy(h_hbm, src_ref[base + r], buf, r, sem).start()
        return carry

    def drain(r, carry):
        _row_copy(h_hbm, 0, buf, r, sem).wait()
        return carry

    lax.fori_loop(0, tg, issue, 0)
    lax.fori_loop(0, tg, drain, 0)
    o_ref[...] = buf[...].astype(o_ref.dtype)


def _gather_rows(h, src, *, tg=512):
    p = src.shape[0]
    d = h.shape[1]
    kern = functools.partial(_gather_kernel, tg=tg)
    grid_spec = pltpu.PrefetchScalarGridSpec(
        num_scalar_prefetch=1,
        grid=(p // tg,),
        in_specs=[pl.BlockSpec(memory_space=pl.ANY)],
        out_specs=pl.BlockSpec((tg, d), lambda i, s: (i, 0)),
        scratch_shapes=[pltpu.VMEM((tg, d), F32), pltpu.SemaphoreType.DMA(())],
    )
    return pl.pallas_call(
        kern,
        grid_spec=grid_spec,
        out_shape=jax.ShapeDtypeStruct((p, d), BF16),
        compiler_params=_params(("arbitrary",)),
    )(src, h)


def _combine_ln_kernel(pos_ref, y_hbm, route_ref, h_ref, g_ref, b_ref, hf_ref, hb_ref,
                       buf0, buf1, sem, *, tm):
    base = pl.program_id(0) * tm

    def issue(r, carry):
        _row_copy(y_hbm, pos_ref[2 * (base + r)], buf0, r, sem).start()
        _row_copy(y_hbm, pos_ref[2 * (base + r) + 1], buf1, r, sem).start()
        return carry

    def drain(r, carry):
        _row_copy(y_hbm, 0, buf0, r, sem).wait()
        _row_copy(y_hbm, 0, buf1, r, sem).wait()
        return carry

    lax.fori_loop(0, tm, issue, 0)
    lax.fori_loop(0, tm, drain, 0)
    route = route_ref[...]
    y = route[:, 2:3] * buf0[...] + route[:, 3:4] * buf1[...]
    out = _layer_norm_rows(ALPHA * h_ref[...] + y, g_ref[...], b_ref[...])
    hf_ref[...] = out
    hb_ref[...] = out.astype(BF16)


def _combine_ln(pos, ys, route, h, g, b, *, tm=256):
    m, d = h.shape
    tm = min(tm, m)
    kern = functools.partial(_combine_ln_kernel, tm=tm)
    row = pl.BlockSpec((tm, d), lambda i, s: (i, 0))
    vec = pl.BlockSpec((1, d), lambda i, s: (0, 0))
    grid_spec = pltpu.PrefetchScalarGridSpec(
        num_scalar_prefetch=1,
        grid=(m // tm,),
        in_specs=[pl.BlockSpec(memory_space=pl.ANY),
                  pl.BlockSpec((tm, LANES), lambda i, s: (i, 0)),
                  row, vec, vec],
        out_specs=[row, row],
        scratch_shapes=[pltpu.VMEM((tm, d), F32), pltpu.VMEM((tm, d), F32),
                        pltpu.SemaphoreType.DMA(())],
    )
    return pl.pallas_call(
        kern,
        grid_spec=grid_spec,
        out_shape=[jax.ShapeDtypeStruct((m, d), F32), jax.ShapeDtypeStruct((m, d), BF16)],
        compiler_params=_params(("arbitrary",)),
    )(pos, ys, route, h, g, b)


def _moe_layer(hf, hb, router, w_gate, w_up, w_down, layer, g, b, *, tm):
    t, d = hf.shape
    tm = min(tm, t)
    route = _router(hf, jnp.pad(router.astype(F32), ((0, 0), (0, LANES - N_EXPERTS))))

    expert = route[:, :TOP_K].astype(jnp.int32).reshape(-1)
    onehot = (expert[:, None] == jnp.arange(N_EXPERTS, dtype=jnp.int32)[None]).astype(jnp.int32)
    csum = jnp.cumsum(onehot, axis=0)
    counts = csum[-1]
    rank = jnp.take_along_axis(csum, expert[:, None], axis=1)[:, 0] - 1
    tiles_per = (counts + tm - 1) // tm
    tile_end = jnp.cumsum(tiles_per)
    start = (tile_end - tiles_per) * tm
    pos = (start[expert] + rank).astype(jnp.int32)
    nt_max = (t * TOP_K) // tm + N_EXPERTS
    n_tiles = tile_end[-1].astype(jnp.int32)
    src = jnp.zeros((nt_max * tm,), jnp.int32).at[pos].set(
        jnp.arange(t * TOP_K, dtype=jnp.int32) // TOP_K)
    tidx = jnp.arange(nt_max, dtype=jnp.int32)
    tile_expert = jnp.searchsorted(tile_end, jnp.minimum(tidx, n_tiles - 1), side="right")
    tile_expert = (jnp.minimum(tile_expert, N_EXPERTS - 1) + layer * N_EXPERTS).astype(jnp.int32)

    xs = _gather_rows(hf, src)
    ys = _ffn(xs, w_gate, w_up, w_down, tile_expert, n_tiles.reshape(1), tm=tm)
    return _combine_ln(pos, ys, route, hf, g, b)


def _dense_layer(hf, hb, w_gate, w_up, w_down, layer, g, b, *, tm):
    t = hf.shape[0]
    tm = min(tm, t)
    nt = t // tm
    ys = _ffn(hb, w_gate, w_up, w_down,
              jnp.full((nt,), layer, jnp.int32), jnp.full((1,), nt, jnp.int32), tm=tm)
    return _res_ln(ys, hf, g, b)


def _to_bf16_kernel(x_ref, o_ref):
    o_ref[...] = x_ref[...].astype(BF16)


def _to_bf16(x, *, tm=512):
    m, n = x.shape
    tm = min(tm, m)
    spec = pl.BlockSpec((tm, n), lambda i: (i, 0))
    return pl.pallas_call(
        _to_bf16_kernel, grid=(m // tm,), in_specs=[spec], out_specs=spec,
        out_shape=jax.ShapeDtypeStruct((m, n), BF16),
        compiler_params=_params(("parallel",)),
    )(x)


def _row(v):
    return v.astype(F32).reshape(1, -1)


def kernel(x, ln_gain, ln_bias, diff_w_in, diff_lambda, diff_subln_gain, diff_w_out,
           fox_w_in, fox_b_f, fox_qk_gain, fox_w_out, swa_w_in, swa_sinks, swa_w_out,
           dense_w_gate, dense_w_up, dense_w_down, moe_router, moe_w_gate, moe_w_up,
           moe_w_down, *, ffn_tm=1024):
    batch, seq, d = x.shape
    t = batch * seq
    hf = x.reshape(t, d).astype(F32)
    hb = _to_bf16(hf)
    depth = ln_gain.shape[0]
    moe_w = [w.reshape((-1,) + w.shape[2:]) for w in (moe_w_gate, moe_w_up, moe_w_down)]
    for i in range(depth):
        kind, j = i % 3, i // 3
        if kind == 0:
            proj = _proj(hb, diff_w_in, j, 3 * D_MODEL)
            o = _diff_attention(proj, diff_lambda[j], diff_subln_gain[j], batch, seq, i)
            w_out = diff_w_out
        elif kind == 1:
            w_in = fox_w_in[j]
            qkv = 3 * D_MODEL
            gain = jnp.concatenate([jnp.tile(fox_qk_gain[j, 0].astype(F32), FOX_HEADS),
                                    jnp.tile(fox_qk_gain[j, 1].astype(F32), FOX_HEADS),
                                    jnp.ones((D_MODEL,), F32)]).reshape(1, qkv)
            proj = _proj(hb, fox_w_in, j, qkv, gain=gain, n_norm=2 * D_MODEL)
            gate = _proj(hb, w_in[None, :, qkv + FOX_HEADS:], 0, D_MODEL, out_dtype=F32)
            wf_t = jnp.pad(w_in[:, qkv:qkv + FOX_HEADS].T, ((0, LANES - FOX_HEADS), (0, 0)))
            bf_col = jnp.pad(fox_b_f[j].astype(F32), (0, LANES - FOX_HEADS)).reshape(LANES, 1)
            ccol = _fox_cumlog(hb, wf_t, bf_col, batch, seq)
            o = _fox_attention(proj, ccol, gate, batch, seq)
            w_out = fox_w_out
        else:
            proj = _proj(hb, swa_w_in, j, SWA_Q_DIM + 2 * SWA_KV_DIM)
            o = _swa_attention(proj, swa_sinks[j], batch, seq)
            w_out = swa_w_out
        hf, hb = _outproj_ln(o, w_out, j, hf, _row(ln_gain[i, 0]), _row(ln_bias[i, 0]))
        c = i // 2
        g, b = _row(ln_gain[i, 1]), _row(ln_bias[i, 1])
        if i % 2 == 0:
            hf, hb = _dense_layer(hf, hb, dense_w_gate, dense_w_up, dense_w_down, c, g, b, tm=ffn_tm)
        else:
            hf, hb = _moe_layer(hf, hb, moe_router[c], *moe_w, c, g, b, tm=ffn_tm)
    return hf.reshape(batch, seq, d)
```

```python
import functools
import math

import numpy as np
import jax
import jax.numpy as jnp
from jax import lax
from jax.experimental import pallas as pl
from jax.experimental.pallas import tpu as pltpu

F32 = jnp.float32
BF16 = jnp.bfloat16

D_MODEL = 2048
DEPTH = 4
LN_EPS = 1e-5
ALPHA = (2 * DEPTH) ** 0.25

DIFF_HEADS = 16
DIFF_HEAD_DIM = 64
FOX_HEADS = 16
FOX_HEAD_DIM = 128
SWA_Q_HEADS = 32
SWA_KV_HEADS = 4
SWA_HEAD_DIM = 64
SWA_BLOCK = 128
SWA_Q_DIM = SWA_Q_HEADS * SWA_HEAD_DIM
SWA_KV_DIM = SWA_KV_HEADS * SWA_HEAD_DIM
N_EXPERTS = 8
TOP_K = 2

LANES = 128
VMEM_LIMIT = 56 * 1024 * 1024

NEG_INF = float("-inf")
_NT = (((1,), (1,)), ((), ()))


def _slopes(n):
    def pow2(m):
        start = 2.0 ** (-8.0 / m)
        return [start ** (i + 1) for i in range(m)]
    if n & (n - 1) == 0:
        s = pow2(n)
    else:
        c = 2 ** int(math.floor(math.log2(n)))
        s = pow2(c) + pow2(2 * c)[0::2][: n - c]
    return jnp.asarray(np.array(s, dtype=np.float32))


def _params(sem, vmem=VMEM_LIMIT):
    return pltpu.CompilerParams(dimension_semantics=sem, vmem_limit_bytes=vmem)


def _layer_norm_rows(y, g, b):
    mu = jnp.mean(y, axis=-1, keepdims=True)
    yc = y - mu
    var = jnp.mean(yc * yc, axis=-1, keepdims=True)
    return yc * lax.rsqrt(var + LN_EPS) * g + b


def _sigmoid(x):
    return 1.0 / (1.0 + jnp.exp(-x))


def _proj_kernel(a_ref, w_ref, g_ref, o_ref, *, n_norm_tiles, tn):
    acc = jnp.dot(a_ref[...], w_ref[...].astype(BF16), preferred_element_type=F32)
    if n_norm_tiles == 0:
        o_ref[...] = acc.astype(o_ref.dtype)
        return
    j = pl.program_id(1)

    @pl.when(j < n_norm_tiles)
    def _():
        for c in range(tn // LANES):
            sl = slice(c * LANES, (c + 1) * LANES)
            blk = acc[:, sl]
            ms = jnp.mean(blk * blk, axis=-1, keepdims=True)
            o_ref[:, sl] = (blk * lax.rsqrt(ms + LN_EPS) * g_ref[:, sl]).astype(o_ref.dtype)

    @pl.when(j >= n_norm_tiles)
    def _():
        o_ref[...] = acc.astype(o_ref.dtype)


def _proj(a, w, layer, n_out, *, gain=None, n_norm=0, out_dtype=BF16, tm=1024, tn=512):
    m, k = a.shape
    tm = min(tm, m)
    tn = min(tn, n_out)
    if gain is None:
        gain = jnp.ones((1, n_out), F32)
    kern = functools.partial(_proj_kernel, n_norm_tiles=n_norm // tn, tn=tn)
    return pl.pallas_call(
        kern,
        grid=(m // tm, n_out // tn),
        in_specs=[pl.BlockSpec((tm, k), lambda i, j: (i, 0)),
                  pl.BlockSpec((None, k, tn), lambda i, j: (layer, 0, j)),
                  pl.BlockSpec((1, tn), lambda i, j: (0, j))],
        out_specs=pl.BlockSpec((tm, tn), lambda i, j: (i, j)),
        out_shape=jax.ShapeDtypeStruct((m, n_out), out_dtype),
        compiler_params=_params(("parallel", "arbitrary")),
    )(a, w, gain)


def _outproj_ln_kernel(o_ref, w_ref, h_ref, g_ref, b_ref, hf_ref, hb_ref, acc_ref):
    k = pl.program_id(1)

    @pl.when(k == 0)
    def _():
        acc_ref[...] = jnp.zeros_like(acc_ref)

    acc_ref[...] += jnp.dot(o_ref[...], w_ref[...].astype(BF16), preferred_element_type=F32)

    @pl.when(k == pl.num_programs(1) - 1)
    def _():
        out = _layer_norm_rows(ALPHA * h_ref[...] + acc_ref[...], g_ref[...], b_ref[...])
        hf_ref[...] = out
        hb_ref[...] = out.astype(BF16)


def _outproj_ln(o, w, layer, h, g, b, *, tm=512, tk=512):
    m, k = o.shape
    n = w.shape[2]
    tm = min(tm, m)
    return pl.pallas_call(
        _outproj_ln_kernel,
        grid=(m // tm, k // tk),
        in_specs=[pl.BlockSpec((tm, tk), lambda i, kk: (i, kk)),
                  pl.BlockSpec((None, tk, n), lambda i, kk: (layer, kk, 0)),
                  pl.BlockSpec((tm, n), lambda i, kk: (i, 0)),
                  pl.BlockSpec((1, n), lambda i, kk: (0, 0)),
                  pl.BlockSpec((1, n), lambda i, kk: (0, 0))],
        out_specs=[pl.BlockSpec((tm, n), lambda i, kk: (i, 0)),
                   pl.BlockSpec((tm, n), lambda i, kk: (i, 0))],
        out_shape=[jax.ShapeDtypeStruct((m, n), F32), jax.ShapeDtypeStruct((m, n), BF16)],
        scratch_shapes=[pltpu.VMEM((tm, n), F32)],
        compiler_params=_params(("parallel", "arbitrary")),
    )(o, w, h, g, b)


def _res_ln_kernel(y_ref, h_ref, g_ref, b_ref, hf_ref, hb_ref):
    out = _layer_norm_rows(ALPHA * h_ref[...] + y_ref[...], g_ref[...], b_ref[...])
    hf_ref[...] = out
    hb_ref[...] = out.astype(BF16)


def _res_ln(y, h, g, b, *, tm=512):
    m, n = h.shape
    tm = min(tm, m)
    row = pl.BlockSpec((tm, n), lambda i: (i, 0))
    vec = pl.BlockSpec((1, n), lambda i: (0, 0))
    return pl.pallas_call(
        _res_ln_kernel,
        grid=(m // tm,),
        in_specs=[row, row, vec, vec],
        out_specs=[row, row],
        out_shape=[jax.ShapeDtypeStruct((m, n), F32), jax.ShapeDtypeStruct((m, n), BF16)],
        compiler_params=_params(("parallel",)),
    )(y, h, g, b)


ATTN_TILE = 512


def _split3(x):
    hi = x.astype(BF16).astype(F32)
    r = x - hi
    mid = r.astype(BF16).astype(F32)
    lo = (r - mid).astype(BF16).astype(F32)
    return hi, mid, lo


def _lane_groups(lane, groups):
    out = jnp.zeros(lane.shape, F32)
    for g, parts in enumerate(groups):
        for i, part in enumerate(parts):
            out = jnp.where(lane == 3 * g + i, part, out)
    return out


def _flash_schedule(qi, scores, absorb, sa_ref, sb_ref):
    scores(0, sa_ref)

    def pair(j, carry):
        scores(2 * j + 1, sb_ref)
        absorb(2 * j, sa_ref, False)
        scores(2 * j + 2, sa_ref)
        absorb(2 * j + 1, sb_ref, False)
        return carry

    lax.fori_loop(0, qi // 2, pair, 0)

    @pl.when(qi % 2 == 1)
    def _():
        scores(qi, sb_ref)
        absorb(qi - 1, sa_ref, False)
        absorb(qi, sb_ref, True)

    @pl.when(qi % 2 == 0)
    def _():
        absorb(qi, sa_ref, True)


def _flash_absorb(s, v, m_ref, l_ref, acc_ref, c2):
    m = m_ref[...]
    m_new = jnp.maximum(m, jnp.max(s, axis=-1, keepdims=True))
    alpha = jnp.exp2((m - m_new) * c2)
    p = jnp.exp2((s - m_new[:, 0:1]) * c2)
    l_ref[...] = alpha * l_ref[...] + jnp.sum(p, axis=-1, keepdims=True)
    acc_ref[...] = alpha * acc_ref[...] + jnp.dot(p.astype(BF16), v, preferred_element_type=F32)
    m_ref[...] = m_new


def _flash_scratch(rows, tk):
    return [pltpu.VMEM((rows, tk), F32), pltpu.VMEM((rows, tk), F32),
            pltpu.VMEM((rows, LANES), F32), pltpu.VMEM((rows, LANES), F32),
            pltpu.VMEM((rows, LANES), F32)]


def _diff_attn_kernel(slopes_ref, q_ref, k_ref, v_ref, lam_ref, gain_ref, o_ref,
                      sa_ref, sb_ref, m_ref, l_ref, acc_ref, *, tq, lam_init):
    h = pl.program_id(1)
    qi = pl.program_id(2)
    half = DIFF_HEAD_DIM

    lane = lax.broadcasted_iota(jnp.int32, (tq, LANES), 1)
    row = lax.broadcasted_iota(jnp.int32, (tq, LANES), 0)
    slope = jnp.full((tq, LANES), slopes_ref[h], F32)
    sl = _split3(slope)
    neg_rows = _split3(-(slope * row.astype(F32)))
    q_bias = _lane_groups(lane, [sl, tuple(256.0 * x for x in sl), tuple(-x for x in sl), neg_rows])
    ones = (1.0, 1.0, 1.0)
    j_lo = (row & 255).astype(F32)
    j_hi = (row >> 8).astype(F32)
    k_bias_const = _lane_groups(lane, [(j_lo,) * 3, (j_hi,) * 3, (0.0,) * 3, ones])
    off_lanes = (lane >= 6) & (lane < 9)

    q = q_ref[...] * jnp.asarray(DIFF_HEAD_DIM ** -0.5, BF16)
    zero = jnp.zeros_like(q)
    qb = q_bias.astype(BF16)
    qs = jnp.concatenate(
        [jnp.concatenate([jnp.where(lane < half, q, zero), qb], axis=1),
         jnp.concatenate([jnp.where(lane < half, zero, q), qb], axis=1)], axis=0)

    irel = lax.broadcasted_iota(jnp.int32, (tq, tq), 0)
    jrel = lax.broadcasted_iota(jnp.int32, (tq, tq), 1)
    causal = jnp.concatenate([irel >= jrel, irel >= jrel], axis=0)

    m_ref[...] = jnp.full(m_ref.shape, NEG_INF, F32)
    l_ref[...] = jnp.zeros(l_ref.shape, F32)
    acc_ref[...] = jnp.zeros(acc_ref.shape, F32)

    def scores(kb, s_ref):
        start = pl.multiple_of(kb * tq, tq)
        off = ((qi - kb) * tq).astype(F32)
        kb_bias = jnp.where(off_lanes, off, k_bias_const).astype(BF16)
        k = jnp.concatenate([k_ref[pl.ds(start, tq), :], kb_bias], axis=1)
        s_ref[...] = lax.dot_general(qs, k, _NT, preferred_element_type=F32)

    def absorb(kb, s_ref, diag):
        s = s_ref[...]
        if diag:
            s = jnp.where(causal, s, NEG_INF)
        v = v_ref[pl.ds(pl.multiple_of(kb * tq, tq), tq), :]
        _flash_absorb(s, v, m_ref, l_ref, acc_ref, math.log2(math.e))

    _flash_schedule(qi, scores, absorb, sa_ref, sb_ref)

    lp = lam_ref[...]
    lam = (jnp.exp(jnp.sum(lp[0:1] * lp[1:2], axis=-1, keepdims=True))
           - jnp.exp(jnp.sum(lp[2:3] * lp[3:4], axis=-1, keepdims=True)) + lam_init)
    o = acc_ref[...] / l_ref[...]
    a = o[:tq] - lam * o[tq:]
    ms = jnp.mean(a * a, axis=-1, keepdims=True)
    y = a * lax.rsqrt(ms + LN_EPS) * gain_ref[...] * (1.0 - lam_init)
    o_ref[...] = y.astype(o_ref.dtype)


def _diff_attention(proj, lam_params, gain, batch, seq, layer_idx, *, tq=ATTN_TILE):
    t = proj.shape[0]
    tq = min(tq, seq)
    nq = seq // tq
    nh = DIFF_HEADS
    lam_init = 0.8 - 0.6 * math.exp(-0.3 * layer_idx)
    kern = functools.partial(_diff_attn_kernel, tq=tq, lam_init=lam_init)
    grid_spec = pltpu.PrefetchScalarGridSpec(
        num_scalar_prefetch=1,
        grid=(batch, nh, nq),
        in_specs=[pl.BlockSpec((tq, LANES), lambda b, h, qi, s: (b * nq + qi, h)),
                  pl.BlockSpec((seq, LANES), lambda b, h, qi, s: (b, nh + h)),
                  pl.BlockSpec((seq, LANES), lambda b, h, qi, s: (b, 2 * nh + h)),
                  pl.BlockSpec((4, DIFF_HEAD_DIM), lambda b, h, qi, s: (0, 0)),
                  pl.BlockSpec((1, LANES), lambda b, h, qi, s: (0, 0))],
        out_specs=pl.BlockSpec((tq, LANES), lambda b, h, qi, s: (b * nq + qi, h)),
        scratch_shapes=_flash_scratch(2 * tq, tq),
    )
    return pl.pallas_call(
        kern,
        grid_spec=grid_spec,
        out_shape=jax.ShapeDtypeStruct((t, D_MODEL), BF16),
        compiler_params=_params(("parallel", "parallel", "arbitrary")),
    )(_slopes(nh), proj, proj, proj, lam_params.astype(F32), gain.astype(F32).reshape(1, LANES))


def _fox_c_kernel(h_ref, wf_ref, bf_ref, ccol_ref, carry_ref, *, tb):
    j = pl.program_id(1)

    @pl.when(j == 0)
    def _():
        carry_ref[...] = jnp.zeros_like(carry_ref)

    x = lax.dot_general(wf_ref[...].astype(BF16), h_ref[...], _NT, preferred_element_type=F32)
    x = x + bf_ref[...]
    lf = jnp.minimum(x, 0.0) - jnp.log1p(jnp.exp(-jnp.abs(x)))
    ii = lax.broadcasted_iota(jnp.int32, (tb, tb), 0)
    jj = lax.broadcasted_iota(jnp.int32, (tb, tb), 1)
    tri = jnp.where(ii <= jj, 1.0, 0.0).astype(BF16)
    pre = sum(jnp.dot(part.astype(BF16), tri, preferred_element_type=F32) for part in _split3(lf))
    c = pre + carry_ref[...]
    ccol_ref[...] = c.T
    carry_ref[...] = c[:, tb - 1:tb]


def _fox_cumlog(hb, wf_t, bf_col, batch, seq, *, tb=512):
    t, d = hb.shape
    tb = min(tb, seq)
    nb = seq // tb
    kern = functools.partial(_fox_c_kernel, tb=tb)
    return pl.pallas_call(
        kern,
        grid=(batch, nb),
        in_specs=[pl.BlockSpec((tb, d), lambda b, j: (b * nb + j, 0)),
                  pl.BlockSpec((LANES, d), lambda b, j: (0, 0)),
                  pl.BlockSpec((LANES, 1), lambda b, j: (0, 0))],
        out_specs=pl.BlockSpec((tb, LANES), lambda b, j: (b * nb + j, 0)),
        out_shape=jax.ShapeDtypeStruct((t, LANES), F32),
        scratch_shapes=[pltpu.VMEM((LANES, 1), F32)],
        compiler_params=_params(("parallel", "arbitrary")),
    )(hb, wf_t, bf_col)


def _fox_attn_kernel(q_ref, k_ref, v_ref, cq_ref, ck_ref, g_ref, o_ref,
                     sa_ref, sb_ref, m_ref, l_ref, acc_ref, *, tq):
    h = pl.program_id(1)
    qi = pl.program_id(2)
    scale = FOX_HEAD_DIM ** -0.5
    c2 = scale * math.log2(math.e)

    lane = lax.broadcasted_iota(jnp.int32, (tq, LANES), 1)
    head = lane == h

    def head_col(x):
        return jnp.sum(jnp.where(head, x, 0.0), axis=1, keepdims=True)

    ci = head_col(cq_ref[...])
    cref = ci[0:1, :]
    ones = (1.0, 1.0, 1.0)
    q_bias = _lane_groups(lane, [_split3(jnp.broadcast_to((ci - cref) * (1.0 / scale), (tq, LANES))), ones])
    qa = jnp.concatenate([q_ref[...], q_bias.astype(BF16)], axis=1)

    irel = lax.broadcasted_iota(jnp.int32, (tq, tq), 0)
    jrel = lax.broadcasted_iota(jnp.int32, (tq, tq), 1)
    causal = irel >= jrel

    m_ref[...] = jnp.full(m_ref.shape, NEG_INF, F32)
    l_ref[...] = jnp.zeros(l_ref.shape, F32)
    acc_ref[...] = jnp.zeros(acc_ref.shape, F32)

    def scores(kb, s_ref):
        start = pl.multiple_of(kb * tq, tq)
        cj = head_col(ck_ref[pl.ds(start, tq), :])
        neg = jnp.broadcast_to((cref - cj) * (1.0 / scale), (tq, LANES))
        k_bias = _lane_groups(lane, [ones, _split3(neg)])
        k = jnp.concatenate([k_ref[pl.ds(start, tq), :], k_bias.astype(BF16)], axis=1)
        s_ref[...] = lax.dot_general(qa, k, _NT, preferred_element_type=F32)

    def absorb(kb, s_ref, diag):
        s = s_ref[...]
        if diag:
            s = jnp.where(causal, s, NEG_INF)
        v = v_ref[pl.ds(pl.multiple_of(kb * tq, tq), tq), :]
        _flash_absorb(s, v, m_ref, l_ref, acc_ref, c2)

    _flash_schedule(qi, scores, absorb, sa_ref, sb_ref)
    o_ref[...] = ((acc_ref[...] / l_ref[...]) * _sigmoid(g_ref[...])).astype(o_ref.dtype)


def _fox_attention(proj, ccol, gate, batch, seq, *, tq=ATTN_TILE):
    t = proj.shape[0]
    tq = min(tq, seq)
    nq = seq // tq
    nh = FOX_HEADS
    kern = functools.partial(_fox_attn_kernel, tq=tq)
    return pl.pallas_call(
        kern,
        grid=(batch, nh, nq),
        in_specs=[pl.BlockSpec((tq, LANES), lambda b, h, qi: (b * nq + qi, h)),
                  pl.BlockSpec((seq, LANES), lambda b, h, qi: (b, nh + h)),
                  pl.BlockSpec((seq, LANES), lambda b, h, qi: (b, 2 * nh + h)),
                  pl.BlockSpec((tq, LANES), lambda b, h, qi: (b * nq + qi, 0)),
                  pl.BlockSpec((seq, LANES), lambda b, h, qi: (b, 0)),
                  pl.BlockSpec((tq, LANES), lambda b, h, qi: (b * nq + qi, h))],
        out_specs=pl.BlockSpec((tq, LANES), lambda b, h, qi: (b * nq + qi, h)),
        out_shape=jax.ShapeDtypeStruct((t, D_MODEL), BF16),
        scratch_shapes=_flash_scratch(tq, tq),
        compiler_params=_params(("parallel", "parallel", "arbitrary")),
    )(proj, proj, proj, ccol, ccol, gate)


def _swa_attn_kernel(sinks_ref, slopes_ref, q_ref, ko_ref, kp_ref, vo_ref, vp_ref, o_ref, *, tq):
    pair = pl.program_id(1)
    qi = pl.program_id(2)
    half = SWA_HEAD_DIM
    hk = pair // (SWA_Q_HEADS // SWA_KV_HEADS // 2)
    par = hk % 2
    tk = tq + SWA_BLOCK

    lane_k = lax.broadcasted_iota(jnp.int32, (tk, LANES), 1)
    keep = jnp.where(lane_k >= half, 1, 0) == par

    def dup(prev_ref, own_ref):
        x = jnp.concatenate([prev_ref[...], own_ref[...]], axis=0).astype(F32)
        return jnp.where(keep, x, pltpu.roll(x, half, 1)).astype(BF16)

    kd = dup(kp_ref, ko_ref)
    vd = dup(vp_ref, vo_ref)

    q = q_ref[...] * jnp.asarray(SWA_HEAD_DIM ** -0.5, BF16)
    lane = lax.broadcasted_iota(jnp.int32, (tq, LANES), 1)
    lo = lane < half
    zero = jnp.zeros_like(q)
    qs = jnp.concatenate([jnp.where(lo, q, zero), jnp.where(lo, zero, q)], axis=0)
    s = lax.dot_general(qs, kd, _NT, preferred_element_type=F32)

    r = lax.broadcasted_iota(jnp.int32, (tq, tk), 0)
    c = lax.broadcasted_iota(jnp.int32, (tq, tk), 1)
    dist = r - c + SWA_BLOCK
    valid = (dist >= 0) & (dist < SWA_BLOCK) & (qi * tq + r - dist >= 0)
    distf = dist.astype(F32)

    outs = []
    for hh in range(2):
        head = 2 * pair + hh
        sh = s[hh * tq:(hh + 1) * tq] - slopes_ref[head] * distf
        sh = jnp.where(valid, sh, NEG_INF)
        sink = sinks_ref[head]
        m = jnp.maximum(jnp.max(sh, axis=-1, keepdims=True), sink)
        p = jnp.exp(sh - m)
        l = jnp.sum(p, axis=-1, keepdims=True) + jnp.exp(sink - m)
        outs.append(jnp.dot(p.astype(BF16), vd, preferred_element_type=F32) / l)
    o_ref[...] = jnp.where(lo, outs[0], outs[1]).astype(o_ref.dtype)


def _swa_attention(proj, sinks, batch, seq, *, tq=256):
    t = proj.shape[0]
    tq = min(tq, seq)
    nq = seq // tq
    npairs = SWA_Q_HEADS // 2
    pairs_per_col = LANES // SWA_HEAD_DIM * (SWA_Q_HEADS // SWA_KV_HEADS) // 2
    kcol = SWA_Q_DIM // LANES
    vcol = (SWA_Q_DIM + SWA_KV_DIM) // LANES
    sub = tq // SWA_BLOCK

    def own(col):
        return lambda b, p, qi, *_: (b * nq + qi, col + p // pairs_per_col)

    def prev(col):
        return lambda b, p, qi, *_: (jnp.maximum((b * nq + qi) * sub - 1, 0), col + p // pairs_per_col)

    grid_spec = pltpu.PrefetchScalarGridSpec(
        num_scalar_prefetch=2,
        grid=(batch, npairs, nq),
        in_specs=[pl.BlockSpec((tq, LANES), lambda b, p, qi, *_: (b * nq + qi, p)),
                  pl.BlockSpec((tq, LANES), own(kcol)),
                  pl.BlockSpec((SWA_BLOCK, LANES), prev(kcol)),
                  pl.BlockSpec((tq, LANES), own(vcol)),
                  pl.BlockSpec((SWA_BLOCK, LANES), prev(vcol))],
        out_specs=pl.BlockSpec((tq, LANES), lambda b, p, qi, *_: (b * nq + qi, p)),
    )
    kern = functools.partial(_swa_attn_kernel, tq=tq)
    return pl.pallas_call(
        kern,
        grid_spec=grid_spec,
        out_shape=jax.ShapeDtypeStruct((t, SWA_Q_DIM), BF16),
        compiler_params=_params(("parallel", "parallel", "arbitrary")),
    )(sinks.astype(F32), _slopes(SWA_Q_HEADS), proj, proj, proj, proj, proj)


def _ffn_kernel(te_ref, nt_ref, x_ref, wg_ref, wu_ref, wd_ref, o_ref):
    t = pl.program_id(0)
    f = pl.program_id(1)
    live = t < nt_ref[0]

    @pl.when(f == 0)
    def _():
        o_ref[...] = jnp.zeros_like(o_ref)

    @pl.when(live)
    def _():
        x = x_ref[...]
        g = jnp.dot(x, wg_ref[0].astype(BF16), preferred_element_type=F32)
        u = jnp.dot(x, wu_ref[0].astype(BF16), preferred_element_type=F32)
        a = (g * _sigmoid(g) * u).astype(BF16)
        o_ref[...] += jnp.dot(a, wd_ref[0].astype(BF16), preferred_element_type=F32)


def _ffn(x, w_gate, w_up, w_down, tile_expert, n_tiles, *, tm, tf=256):
    p, d = x.shape
    n_exp, _, ff = w_gate.shape
    nf = ff // tf
    nt_max = p // tm

    def xmap(t, f, te, nt):
        return (jnp.minimum(t, nt[0] - 1), 0)

    def fcol(t, f, nt):
        return jnp.where(t < nt[0], f, nf - 1)

    grid_spec = pltpu.PrefetchScalarGridSpec(
        num_scalar_prefetch=2,
        grid=(nt_max, nf),
        in_specs=[pl.BlockSpec((tm, d), xmap),
                  pl.BlockSpec((1, d, tf), lambda t, f, te, nt: (te[t], 0, fcol(t, f, nt))),
                  pl.BlockSpec((1, d, tf), lambda t, f, te, nt: (te[t], 0, fcol(t, f, nt))),
                  pl.BlockSpec((1, tf, d), lambda t, f, te, nt: (te[t], fcol(t, f, nt), 0))],
        out_specs=pl.BlockSpec((tm, d), lambda t, f, te, nt: (t, 0)),
    )
    return pl.pallas_call(
        _ffn_kernel,
        grid_spec=grid_spec,
        out_shape=jax.ShapeDtypeStruct((p, d), F32),
        compiler_params=_params(("arbitrary", "arbitrary")),
    )(tile_expert, n_tiles, x, w_gate, w_up, w_down)


def _router_kernel(h_ref, r_ref, o_ref):
    def split(x):
        hi = x.astype(BF16)
        return hi, (x - hi.astype(F32)).astype(BF16)

    h_hi, h_lo = split(h_ref[...])
    r_hi, r_lo = split(r_ref[...])
    logits = (jnp.dot(h_hi, r_hi, preferred_element_type=F32)
              + jnp.dot(h_hi, r_lo, preferred_element_type=F32)
              + jnp.dot(h_lo, r_hi, preferred_element_type=F32))
    lane = lax.broadcasted_iota(jnp.int32, logits.shape, 1)
    l1 = jnp.where(lane < N_EXPERTS, logits, NEG_INF)
    m1 = jnp.max(l1, axis=-1, keepdims=True)
    i1 = jnp.min(jnp.where(l1 == m1, lane, LANES), axis=-1, keepdims=True)
    l2 = jnp.where(lane == i1, NEG_INF, l1)
    m2 = jnp.max(l2, axis=-1, keepdims=True)
    i2 = jnp.min(jnp.where(l2 == m2, lane, LANES), axis=-1, keepdims=True)
    e = jnp.exp(m2 - m1)
    g1 = 1.0 / (1.0 + e)
    g2 = e / (1.0 + e)
    out = jnp.where(lane == 0, i1.astype(F32),
                    jnp.where(lane == 1, i2.astype(F32),
                              jnp.where(lane == 2, g1, jnp.where(lane == 3, g2, 0.0))))
    o_ref[...] = out


def _router(h, router_padded, *, tm=512):
    m, d = h.shape
    tm = min(tm, m)
    return pl.pallas_call(
        _router_kernel,
        grid=(m // tm,),
        in_specs=[pl.BlockSpec((tm, d), lambda i: (i, 0)),
                  pl.BlockSpec((d, LANES), lambda i: (0, 0))],
        out_specs=pl.BlockSpec((tm, LANES), lambda i: (i, 0)),
        out_shape=jax.ShapeDtypeStruct((m, LANES), F32),
        compiler_params=_params(("parallel",)),
    )(h, router_padded)


def _row_copy(src_hbm, row, dst, slot, sem):
    return pltpu.make_async_copy(src_hbm.at[pl.ds(row, 1), :], dst.at[pl.ds(slot, 1), :], sem)


def _gather_kernel(src_ref, h_hbm, o_ref, buf, sem, *, tg):
    base = pl.program_id(0) * tg

    def issue(r, carry):
        _row_copy(h_hbm, src_ref[base + r], buf, r, sem).start()
        return carry

    def drain(r, carry):
        _row_copy(h_hbm, 0, buf, r, sem).wait()
        return carry

    lax.fori_loop(0, tg, issue, 0)
    lax.fori_loop(0, tg, drain, 0)
    o_ref[...] = buf[...].astype(o_ref.dtype)


def _gather_rows(h, src, *, tg=512):
    p = src.shape[0]
    d = h.shape[1]
    kern = functools.partial(_gather_kernel, tg=tg)
    grid_spec = pltpu.PrefetchScalarGridSpec(
        num_scalar_prefetch=1,
        grid=(p // tg,),
        in_specs=[pl.BlockSpec(memory_space=pl.ANY)],
        out_specs=pl.BlockSpec((tg, d), lambda i, s: (i, 0)),
        scratch_shapes=[pltpu.VMEM((tg, d), F32), pltpu.SemaphoreType.DMA(())],
    )
    return pl.pallas_call(
        kern,
        grid_spec=grid_spec,
        out_shape=jax.ShapeDtypeStruct((p, d), BF16),
        compiler_params=_params(("arbitrary",)),
    )(src, h)


def _combine_ln_kernel(pos_ref, y_hbm, route_ref, h_ref, g_ref, b_ref, hf_ref, hb_ref,
                       buf0, buf1, sem, *, tm):
    base = pl.program_id(0) * tm

    def issue(r, carry):
        _row_copy(y_hbm, pos_ref[2 * (base + r)], buf0, r, sem).start()
        _row_copy(y_hbm, pos_ref[2 * (base + r) + 1], buf1, r, sem).start()
        return carry

    def drain(r, carry):
        _row_copy(y_hbm, 0, buf0, r, sem).wait()
        _row_copy(y_hbm, 0, buf1, r, sem).wait()
        return carry

    lax.fori_loop(0, tm, issue, 0)
    lax.fori_loop(0, tm, drain, 0)
    route = route_ref[...]
    y = route[:, 2:3] * buf0[...] + route[:, 3:4] * buf1[...]
    out = _layer_norm_rows(ALPHA * h_ref[...] + y, g_ref[...], b_ref[...])
    hf_ref[...] = out
    hb_ref[...] = out.astype(BF16)


def _combine_ln(pos, ys, route, h, g, b, *, tm=256):
    m, d = h.shape
    tm = min(tm, m)
    kern = functools.partial(_combine_ln_kernel, tm=tm)
    row = pl.BlockSpec((tm, d), lambda i, s: (i, 0))
    vec = pl.BlockSpec((1, d), lambda i, s: (0, 0))
    grid_spec = pltpu.PrefetchScalarGridSpec(
        num_scalar_prefetch=1,
        grid=(m // tm,),
        in_specs=[pl.BlockSpec(memory_space=pl.ANY),
                  pl.BlockSpec((tm, LANES), lambda i, s: (i, 0)),
                  row, vec, vec],
        out_specs=[row, row],
        scratch_shapes=[pltpu.VMEM((tm, d), F32), pltpu.VMEM((tm, d), F32),
                        pltpu.SemaphoreType.DMA(())],
    )
    return pl.pallas_call(
        kern,
        grid_spec=grid_spec,
        out_shape=[jax.ShapeDtypeStruct((m, d), F32), jax.ShapeDtypeStruct((m, d), BF16)],
        compiler_params=_params(("arbitrary",)),
    )(pos, ys, route, h, g, b)


def _moe_layer(hf, hb, router, w_gate, w_up, w_down, layer, g, b, *, tm):
    t, d = hf.shape
    tm = min(tm, t)
    route = _router(hf, jnp.pad(router.astype(F32), ((0, 0), (0, LANES - N_EXPERTS))))

    expert = route[:, :TOP_K].astype(jnp.int32).reshape(-1)
    onehot = (expert[:, None] == jnp.arange(N_EXPERTS, dtype=jnp.int32)[None]).astype(jnp.int32)
    csum = jnp.cumsum(onehot, axis=0)
    counts = csum[-1]
    rank = jnp.take_along_axis(csum, expert[:, None], axis=1)[:, 0] - 1
    tiles_per = (counts + tm - 1) // tm
    tile_end = jnp.cumsum(tiles_per)
    start = (tile_end - tiles_per) * tm
    pos = (start[expert] + rank).astype(jnp.int32)
    nt_max = (t * TOP_K) // tm + N_EXPERTS
    n_tiles = tile_end[-1].astype(jnp.int32)
    src = jnp.zeros((nt_max * tm,), jnp.int32).at[pos].set(
        jnp.arange(t * TOP_K, dtype=jnp.int32) // TOP_K)
    tidx = jnp.arange(nt_max, dtype=jnp.int32)
    tile_expert = jnp.searchsorted(tile_end, jnp.minimum(tidx, n_tiles - 1), side="right")
    tile_expert = (jnp.minimum(tile_expert, N_EXPERTS - 1) + layer * N_EXPERTS).astype(jnp.int32)

    xs = _gather_rows(hf, src)
    ys = _ffn(xs, w_gate, w_up, w_down, tile_expert, n_tiles.reshape(1), tm=tm)
    return _combine_ln(pos, ys, route, hf, g, b)


def _dense_layer(hf, hb, w_gate, w_up, w_down, layer, g, b, *, tm):
    t = hf.shape[0]
    tm = min(tm, t)
    nt = t // tm
    ys = _ffn(hb, w_gate, w_up, w_down,
              jnp.full((nt,), layer, jnp.int32), jnp.full((1,), nt, jnp.int32), tm=tm)
    return _res_ln(ys, hf, g, b)


def _to_bf16_kernel(x_ref, o_ref):
    o_ref[...] = x_ref[...].astype(BF16)


def _to_bf16(x, *, tm=512):
    m, n = x.shape
    tm = min(tm, m)
    spec = pl.BlockSpec((tm, n), lambda i: (i, 0))
    return pl.pallas_call(
        _to_bf16_kernel, grid=(m // tm,), in_specs=[spec], out_specs=spec,
        out_shape=jax.ShapeDtypeStruct((m, n), BF16),
        compiler_params=_params(("parallel",)),
    )(x)


def _row(v):
    return v.astype(F32).reshape(1, -1)


def kernel(x, ln_gain, ln_bias, diff_w_in, diff_lambda, diff_subln_gain, diff_w_out,
           fox_w_in, fox_b_f, fox_qk_gain, fox_w_out, swa_w_in, swa_sinks, swa_w_out,
           dense_w_gate, dense_w_up, dense_w_down, moe_router, moe_w_gate, moe_w_up,
           moe_w_down, *, ffn_tm=1024):
    batch, seq, d = x.shape
    t = batch * seq
    hf = x.reshape(t, d).astype(F32)
    hb = _to_bf16(hf)
    depth = ln_gain.shape[0]
    moe_w = [w.reshape((-1,) + w.shape[2:]) for w in (moe_w_gate, moe_w_up, moe_w_down)]
    for i in range(depth):
        kind, j = i % 3, i // 3
        if kind == 0:
            proj = _proj(hb, diff_w_in, j, 3 * D_MODEL)
            o = _diff_attention(proj, diff_lambda[j], diff_subln_gain[j], batch, seq, i)
            w_out = diff_w_out
        elif kind == 1:
            w_in = fox_w_in[j]
            qkv = 3 * D_MODEL
            gain = jnp.concatenate([jnp.tile(fox_qk_gain[j, 0].astype(F32), FOX_HEADS),
                                    jnp.tile(fox_qk_gain[j, 1].astype(F32), FOX_HEADS),
                                    jnp.ones((D_MODEL,), F32)]).reshape(1, qkv)
            proj = _proj(hb, fox_w_in, j, qkv, gain=gain, n_norm=2 * D_MODEL)
            gate = _proj(hb, w_in[None, :, qkv + FOX_HEADS:], 0, D_MODEL, out_dtype=F32)
            wf_t = jnp.pad(w_in[:, qkv:qkv + FOX_HEADS].T, ((0, LANES - FOX_HEADS), (0, 0)))
            bf_col = jnp.pad(fox_b_f[j].astype(F32), (0, LANES - FOX_HEADS)).reshape(LANES, 1)
            ccol = _fox_cumlog(hb, wf_t, bf_col, batch, seq)
            o = _fox_attention(proj, ccol, gate, batch, seq)
            w_out = fox_w_out
        else:
            proj = _proj(hb, swa_w_in, j, SWA_Q_DIM + 2 * SWA_KV_DIM)
            o = _swa_attention(proj, swa_sinks[j], batch, seq)
            w_out = swa_w_out
        hf, hb = _outproj_ln(o, w_out, j, hf, _row(ln_gain[i, 0]), _row(ln_bias[i, 0]))
        c = i // 2
        g, b = _row(ln_gain[i, 1]), _row(ln_bias[i, 1])
        if i % 2 == 0:
            hf, hb = _dense_layer(hf, hb, dense_w_gate, dense_w_up, dense_w_down, c, g, b, tm=ffn_tm)
        else:
            hf, hb = _moe_layer(hf, hb, moe_router[c], *moe_w, c, g, b, tm=ffn_tm)
    return hf.reshape(batch, seq, d)
```

```python
import functools
import math

import numpy as np
import jax
import jax.numpy as jnp
from jax import lax
from jax.experimental import pallas as pl
from jax.experimental.pallas import tpu as pltpu

F32 = jnp.float32
BF16 = jnp.bfloat16

D_MODEL = 2048
DEPTH = 4
LN_EPS = 1e-5
ALPHA = (2 * DEPTH) ** 0.25

DIFF_HEADS = 16
DIFF_HEAD_DIM = 64
FOX_HEADS = 16
FOX_HEAD_DIM = 128
SWA_Q_HEADS = 32
SWA_KV_HEADS = 4
SWA_HEAD_DIM = 64
SWA_BLOCK = 128
SWA_Q_DIM = SWA_Q_HEADS * SWA_HEAD_DIM
SWA_KV_DIM = SWA_KV_HEADS * SWA_HEAD_DIM
N_EXPERTS = 8
TOP_K = 2

LANES = 128
VMEM_LIMIT = 56 * 1024 * 1024

NEG_INF = float("-inf")
_NT = (((1,), (1,)), ((), ()))


def _slopes(n):
    def pow2(m):
        start = 2.0 ** (-8.0 / m)
        return [start ** (i + 1) for i in range(m)]
    if n & (n - 1) == 0:
        s = pow2(n)
    else:
        c = 2 ** int(math.floor(math.log2(n)))
        s = pow2(c) + pow2(2 * c)[0::2][: n - c]
    return jnp.asarray(np.array(s, dtype=np.float32))


def _params(sem, vmem=VMEM_LIMIT):
    return pltpu.CompilerParams(dimension_semantics=sem, vmem_limit_bytes=vmem)


def _layer_norm_rows(y, g, b):
    mu = jnp.mean(y, axis=-1, keepdims=True)
    yc = y - mu
    var = jnp.mean(yc * yc, axis=-1, keepdims=True)
    return yc * lax.rsqrt(var + LN_EPS) * g + b


def _sigmoid(x):
    return 1.0 / (1.0 + jnp.exp(-x))


def _proj_kernel(a_ref, w_ref, g_ref, o_ref, *, n_norm_tiles, tn):
    acc = jnp.dot(a_ref[...], w_ref[...].astype(BF16), preferred_element_type=F32)
    if n_norm_tiles == 0:
        o_ref[...] = acc.astype(o_ref.dtype)
        return
    j = pl.program_id(1)

    @pl.when(j < n_norm_tiles)
    def _():
        for c in range(tn // LANES):
            sl = slice(c * LANES, (c + 1) * LANES)
            blk = acc[:, sl]
            ms = jnp.mean(blk * blk, axis=-1, keepdims=True)
            o_ref[:, sl] = (blk * lax.rsqrt(ms + LN_EPS) * g_ref[:, sl]).astype(o_ref.dtype)

    @pl.when(j >= n_norm_tiles)
    def _():
        o_ref[...] = acc.astype(o_ref.dtype)


def _proj(a, w, layer, n_out, *, gain=None, n_norm=0, out_dtype=BF16, tm=1024, tn=512):
    m, k = a.shape
    tm = min(tm, m)
    tn = min(tn, n_out)
    if gain is None:
        gain = jnp.ones((1, n_out), F32)
    kern = functools.partial(_proj_kernel, n_norm_tiles=n_norm // tn, tn=tn)
    return pl.pallas_call(
        kern,
        grid=(m // tm, n_out // tn),
        in_specs=[pl.BlockSpec((tm, k), lambda i, j: (i, 0)),
                  pl.BlockSpec((None, k, tn), lambda i, j: (layer, 0, j)),
                  pl.BlockSpec((1, tn), lambda i, j: (0, j))],
        out_specs=pl.BlockSpec((tm, tn), lambda i, j: (i, j)),
        out_shape=jax.ShapeDtypeStruct((m, n_out), out_dtype),
        compiler_params=_params(("parallel", "arbitrary")),
    )(a, w, gain)


def _outproj_ln_kernel(o_ref, w_ref, h_ref, g_ref, b_ref, hf_ref, hb_ref, acc_ref):
    k = pl.program_id(1)

    @pl.when(k == 0)
    def _():
        acc_ref[...] = jnp.zeros_like(acc_ref)

    acc_ref[...] += jnp.dot(o_ref[...], w_ref[...].astype(BF16), preferred_element_type=F32)

    @pl.when(k == pl.num_programs(1) - 1)
    def _():
        out = _layer_norm_rows(ALPHA * h_ref[...] + acc_ref[...], g_ref[...], b_ref[...])
        hf_ref[...] = out
        hb_ref[...] = out.astype(BF16)


def _outproj_ln(o, w, layer, h, g, b, *, tm=512, tk=512):
    m, k = o.shape
    n = w.shape[2]
    tm = min(tm, m)
    return pl.pallas_call(
        _outproj_ln_kernel,
        grid=(m // tm, k // tk),
        in_specs=[pl.BlockSpec((tm, tk), lambda i, kk: (i, kk)),
                  pl.BlockSpec((None, tk, n), lambda i, kk: (layer, kk, 0)),
                  pl.BlockSpec((tm, n), lambda i, kk: (i, 0)),
                  pl.BlockSpec((1, n), lambda i, kk: (0, 0)),
                  pl.BlockSpec((1, n), lambda i, kk: (0, 0))],
        out_specs=[pl.BlockSpec((tm, n), lambda i, kk: (i, 0)),
                   pl.BlockSpec((tm, n), lambda i, kk: (i, 0))],
        out_shape=[jax.ShapeDtypeStruct((m, n), F32), jax.ShapeDtypeStruct((m, n), BF16)],
        scratch_shapes=[pltpu.VMEM((tm, n), F32)],
        compiler_params=_params(("parallel", "arbitrary")),
    )(o, w, h, g, b)


def _res_ln_kernel(y_ref, h_ref, g_ref, b_ref, hf_ref, hb_ref):
    out = _layer_norm_rows(ALPHA * h_ref[...] + y_ref[...], g_ref[...], b_ref[...])
    hf_ref[...] = out
    hb_ref[...] = out.astype(BF16)


def _res_ln(y, h, g, b, *, tm=512):
    m, n = h.shape
    tm = min(tm, m)
    row = pl.BlockSpec((tm, n), lambda i: (i, 0))
    vec = pl.BlockSpec((1, n), lambda i: (0, 0))
    return pl.pallas_call(
        _res_ln_kernel,
        grid=(m // tm,),
        in_specs=[row, row, vec, vec],
        out_specs=[row, row],
        out_shape=[jax.ShapeDtypeStruct((m, n), F32), jax.ShapeDtypeStruct((m, n), BF16)],
        compiler_params=_params(("parallel",)),
    )(y, h, g, b)


ATTN_TILE = 512


def _split3(x):
    hi = x.astype(BF16).astype(F32)
    r = x - hi
    mid = r.astype(BF16).astype(F32)
    lo = (r - mid).astype(BF16).astype(F32)
    return hi, mid, lo


def _lane_groups(lane, groups):
    out = jnp.zeros(lane.shape, F32)
    for g, parts in enumerate(groups):
        for i, part in enumerate(parts):
            out = jnp.where(lane == 3 * g + i, part, out)
    return out


def _flash_schedule(qi, scores, absorb, sa_ref, sb_ref):
    scores(0, sa_ref)

    def pair(j, carry):
        scores(2 * j + 1, sb_ref)
        absorb(2 * j, sa_ref, False)
        scores(2 * j + 2, sa_ref)
        absorb(2 * j + 1, sb_ref, False)
        return carry

    lax.fori_loop(0, qi // 2, pair, 0)

    @pl.when(qi % 2 == 1)
    def _():
        scores(qi, sb_ref)
        absorb(qi - 1, sa_ref, False)
        absorb(qi, sb_ref, True)

    @pl.when(qi % 2 == 0)
    def _():
        absorb(qi, sa_ref, True)


def _flash_absorb_t(st, v, m_ref, l_ref, acc_ref, c2):
    m = m_ref[...]
    m_new = jnp.maximum(m, jnp.max(st, axis=0, keepdims=True))
    alpha = jnp.exp2((m - m_new) * c2)
    p = jnp.exp2((st - m_new) * c2)
    l_ref[...] = alpha * l_ref[...] + jnp.sum(p, axis=0, keepdims=True)
    pv = lax.dot_general(v, p.astype(BF16), (((0,), (0,)), ((), ())), preferred_element_type=F32)
    acc_ref[...] = alpha * acc_ref[...] + pv
    m_ref[...] = m_new


def _flash_scratch_t(tk, cols):
    return [pltpu.VMEM((tk, cols), F32), pltpu.VMEM((tk, cols), F32),
            pltpu.VMEM((1, cols), F32), pltpu.VMEM((1, cols), F32),
            pltpu.VMEM((LANES, cols), F32)]


def _diff_attn_kernel(slopes_ref, q_ref, k_ref, v_ref, lam_ref, gain_ref, o_ref,
                      sa_ref, sb_ref, m_ref, l_ref, acc_ref, *, tq, lam_init):
    h = pl.program_id(1)
    qi = pl.program_id(2)
    half = DIFF_HEAD_DIM

    lane = lax.broadcasted_iota(jnp.int32, (tq, LANES), 1)
    row = lax.broadcasted_iota(jnp.int32, (tq, LANES), 0)
    slope = jnp.full((tq, LANES), slopes_ref[h], F32)
    sl = _split3(slope)
    neg_rows = _split3(-(slope * row.astype(F32)))
    q_bias = _lane_groups(lane, [sl, tuple(256.0 * x for x in sl), tuple(-x for x in sl), neg_rows])
    ones = (1.0, 1.0, 1.0)
    j_lo = (row & 255).astype(F32)
    j_hi = (row >> 8).astype(F32)
    k_bias_const = _lane_groups(lane, [(j_lo,) * 3, (j_hi,) * 3, (0.0,) * 3, ones])
    off_lanes = (lane >= 6) & (lane < 9)

    q = q_ref[...] * jnp.asarray(DIFF_HEAD_DIM ** -0.5, BF16)
    zero = jnp.zeros_like(q)
    qb = q_bias.astype(BF16)
    qs = jnp.concatenate(
        [jnp.concatenate([jnp.where(lane < half, q, zero), qb], axis=1),
         jnp.concatenate([jnp.where(lane < half, zero, q), qb], axis=1)], axis=0)

    krow = lax.broadcasted_iota(jnp.int32, (tq, tq), 0)
    qcol = lax.broadcasted_iota(jnp.int32, (tq, tq), 1)
    causal = jnp.concatenate([qcol >= krow, qcol >= krow], axis=1)

    m_ref[...] = jnp.full(m_ref.shape, NEG_INF, F32)
    l_ref[...] = jnp.zeros(l_ref.shape, F32)
    acc_ref[...] = jnp.zeros(acc_ref.shape, F32)

    def scores(kb, s_ref):
        start = pl.multiple_of(kb * tq, tq)
        off = ((qi - kb) * tq).astype(F32)
        kb_bias = jnp.where(off_lanes, off, k_bias_const).astype(BF16)
        k = jnp.concatenate([k_ref[pl.ds(start, tq), :], kb_bias], axis=1)
        s_ref[...] = lax.dot_general(k, qs, _NT, preferred_element_type=F32)

    def absorb(kb, s_ref, diag):
        s = s_ref[...]
        if diag:
            s = jnp.where(causal, s, NEG_INF)
        v = v_ref[pl.ds(pl.multiple_of(kb * tq, tq), tq), :]
        _flash_absorb_t(s, v, m_ref, l_ref, acc_ref, math.log2(math.e))

    _flash_schedule(qi, scores, absorb, sa_ref, sb_ref)

    lp = lam_ref[...]
    lam = (jnp.exp(jnp.sum(lp[0:1] * lp[1:2], axis=-1, keepdims=True))
           - jnp.exp(jnp.sum(lp[2:3] * lp[3:4], axis=-1, keepdims=True)) + lam_init)
    o = acc_ref[...] / l_ref[...]
    a = o[:, :tq] - lam * o[:, tq:]
    ms = jnp.mean(a * a, axis=0, keepdims=True)
    y = a * lax.rsqrt(ms + LN_EPS) * gain_ref[...] * (1.0 - lam_init)
    o_ref[...] = y.T.astype(o_ref.dtype)


def _diff_attention(proj, lam_params, gain, batch, seq, layer_idx, *, tq=ATTN_TILE):
    t = proj.shape[0]
    tq = min(tq, seq)
    nq = seq // tq
    nh = DIFF_HEADS
    lam_init = 0.8 - 0.6 * math.exp(-0.3 * layer_idx)
    kern = functools.partial(_diff_attn_kernel, tq=tq, lam_init=lam_init)
    grid_spec = pltpu.PrefetchScalarGridSpec(
        num_scalar_prefetch=1,
        grid=(batch, nh, nq),
        in_specs=[pl.BlockSpec((tq, LANES), lambda b, h, qi, s: (b * nq + qi, h)),
                  pl.BlockSpec((seq, LANES), lambda b, h, qi, s: (b, nh + h)),
                  pl.BlockSpec((seq, LANES), lambda b, h, qi, s: (b, 2 * nh + h)),
                  pl.BlockSpec((4, DIFF_HEAD_DIM), lambda b, h, qi, s: (0, 0)),
                  pl.BlockSpec((LANES, 1), lambda b, h, qi, s: (0, 0))],
        out_specs=pl.BlockSpec((tq, LANES), lambda b, h, qi, s: (b * nq + qi, h)),
        scratch_shapes=_flash_scratch_t(tq, 2 * tq),
    )
    return pl.pallas_call(
        kern,
        grid_spec=grid_spec,
        out_shape=jax.ShapeDtypeStruct((t, D_MODEL), BF16),
        compiler_params=_params(("parallel", "parallel", "arbitrary")),
    )(_slopes(nh), proj, proj, proj, lam_params.astype(F32), gain.astype(F32).reshape(LANES, 1))


def _fox_c_kernel(h_ref, wf_ref, bf_ref, ccol_ref, carry_ref, *, tb):
    j = pl.program_id(1)

    @pl.when(j == 0)
    def _():
        carry_ref[...] = jnp.zeros_like(carry_ref)

    x = lax.dot_general(wf_ref[...].astype(BF16), h_ref[...], _NT, preferred_element_type=F32)
    x = x + bf_ref[...]
    lf = jnp.minimum(x, 0.0) - jnp.log1p(jnp.exp(-jnp.abs(x)))
    ii = lax.broadcasted_iota(jnp.int32, (tb, tb), 0)
    jj = lax.broadcasted_iota(jnp.int32, (tb, tb), 1)
    tri = jnp.where(ii <= jj, 1.0, 0.0).astype(BF16)
    pre = sum(jnp.dot(part.astype(BF16), tri, preferred_element_type=F32) for part in _split3(lf))
    c = pre + carry_ref[...]
    ccol_ref[...] = c.T
    carry_ref[...] = c[:, tb - 1:tb]


def _fox_cumlog(hb, wf_t, bf_col, batch, seq, *, tb=512):
    t, d = hb.shape
    tb = min(tb, seq)
    nb = seq // tb
    kern = functools.partial(_fox_c_kernel, tb=tb)
    return pl.pallas_call(
        kern,
        grid=(batch, nb),
        in_specs=[pl.BlockSpec((tb, d), lambda b, j: (b * nb + j, 0)),
                  pl.BlockSpec((LANES, d), lambda b, j: (0, 0)),
                  pl.BlockSpec((LANES, 1), lambda b, j: (0, 0))],
        out_specs=pl.BlockSpec((tb, LANES), lambda b, j: (b * nb + j, 0)),
        out_shape=jax.ShapeDtypeStruct((t, LANES), F32),
        scratch_shapes=[pltpu.VMEM((LANES, 1), F32)],
        compiler_params=_params(("parallel", "arbitrary")),
    )(hb, wf_t, bf_col)


def _fox_attn_kernel(q_ref, k_ref, v_ref, cq_ref, ck_ref, g_ref, o_ref,
                     sa_ref, sb_ref, m_ref, l_ref, acc_ref, *, tq):
    h = pl.program_id(1)
    qi = pl.program_id(2)
    scale = FOX_HEAD_DIM ** -0.5
    c2 = scale * math.log2(math.e)

    lane = lax.broadcasted_iota(jnp.int32, (tq, LANES), 1)
    head = lane == h

    def head_col(x):
        return jnp.sum(jnp.where(head, x, 0.0), axis=1, keepdims=True)

    ci = head_col(cq_ref[...])
    cref = ci[0:1, :]
    ones = (1.0, 1.0, 1.0)
    q_bias = _lane_groups(lane, [_split3(jnp.broadcast_to((ci - cref) * (1.0 / scale), (tq, LANES))), ones])
    qa = jnp.concatenate([q_ref[...], q_bias.astype(BF16)], axis=1)

    krow = lax.broadcasted_iota(jnp.int32, (tq, tq), 0)
    qcol = lax.broadcasted_iota(jnp.int32, (tq, tq), 1)
    causal = qcol >= krow

    m_ref[...] = jnp.full(m_ref.shape, NEG_INF, F32)
    l_ref[...] = jnp.zeros(l_ref.shape, F32)
    acc_ref[...] = jnp.zeros(acc_ref.shape, F32)

    def scores(kb, s_ref):
        start = pl.multiple_of(kb * tq, tq)
        cj = head_col(ck_ref[pl.ds(start, tq), :])
        neg = jnp.broadcast_to((cref - cj) * (1.0 / scale), (tq, LANES))
        k_bias = _lane_groups(lane, [ones, _split3(neg)])
        k = jnp.concatenate([k_ref[pl.ds(start, tq), :], k_bias.astype(BF16)], axis=1)
        s_ref[...] = lax.dot_general(k, qa, _NT, preferred_element_type=F32)

    def absorb(kb, s_ref, diag):
        s = s_ref[...]
        if diag:
            s = jnp.where(causal, s, NEG_INF)
        v = v_ref[pl.ds(pl.multiple_of(kb * tq, tq), tq), :]
        _flash_absorb_t(s, v, m_ref, l_ref, acc_ref, c2)

    _flash_schedule(qi, scores, absorb, sa_ref, sb_ref)
    o_ref[...] = ((acc_ref[...] / l_ref[...]).T * _sigmoid(g_ref[...])).astype(o_ref.dtype)


def _fox_attention(proj, ccol, gate, batch, seq, *, tq=ATTN_TILE):
    t = proj.shape[0]
    tq = min(tq, seq)
    nq = seq // tq
    nh = FOX_HEADS
    kern = functools.partial(_fox_attn_kernel, tq=tq)
    return pl.pallas_call(
        kern,
        grid=(batch, nh, nq),
        in_specs=[pl.BlockSpec((tq, LANES), lambda b, h, qi: (b * nq + qi, h)),
                  pl.BlockSpec((seq, LANES), lambda b, h, qi: (b, nh + h)),
                  pl.BlockSpec((seq, LANES), lambda b, h, qi: (b, 2 * nh + h)),
                  pl.BlockSpec((tq, LANES), lambda b, h, qi: (b * nq + qi, 0)),
                  pl.BlockSpec((seq, LANES), lambda b, h, qi: (b, 0)),
                  pl.BlockSpec((tq, LANES), lambda b, h, qi: (b * nq + qi, h))],
        out_specs=pl.BlockSpec((tq, LANES), lambda b, h, qi: (b * nq + qi, h)),
        out_shape=jax.ShapeDtypeStruct((t, D_MODEL), BF16),
        scratch_shapes=_flash_scratch_t(tq, tq),
        compiler_params=_params(("parallel", "parallel", "arbitrary")),
    )(proj, proj, proj, ccol, ccol, gate)


def _swa_attn_kernel(sinks_ref, slopes_ref, q_ref, ko_ref, kp_ref, vo_ref, vp_ref, o_ref, *, tq):
    pair = pl.program_id(1)
    qi = pl.program_id(2)
    half = SWA_HEAD_DIM
    hk = pair // (SWA_Q_HEADS // SWA_KV_HEADS // 2)
    par = hk % 2
    tk = tq + SWA_BLOCK

    lane_k = lax.broadcasted_iota(jnp.int32, (tk, LANES), 1)
    keep = jnp.where(lane_k >= half, 1, 0) == par

    def dup(prev_ref, own_ref):
        x = jnp.concatenate([prev_ref[...], own_ref[...]], axis=0).astype(F32)
        return jnp.where(keep, x, pltpu.roll(x, half, 1)).astype(BF16)

    kd = dup(kp_ref, ko_ref)
    vd = dup(vp_ref, vo_ref)

    q = q_ref[...] * jnp.asarray(SWA_HEAD_DIM ** -0.5, BF16)
    lane = lax.broadcasted_iota(jnp.int32, (tq, LANES), 1)
    lo = lane < half
    zero = jnp.zeros_like(q)
    qs = jnp.concatenate([jnp.where(lo, q, zero), jnp.where(lo, zero, q)], axis=0)
    s = lax.dot_general(qs, kd, _NT, preferred_element_type=F32)

    r = lax.broadcasted_iota(jnp.int32, (tq, tk), 0)
    c = lax.broadcasted_iota(jnp.int32, (tq, tk), 1)
    dist = r - c + SWA_BLOCK
    valid = (dist >= 0) & (dist < SWA_BLOCK) & (qi * tq + r - dist >= 0)
    distf = dist.astype(F32)

    outs = []
    for hh in range(2):
        head = 2 * pair + hh
        sh = s[hh * tq:(hh + 1) * tq] - slopes_ref[head] * distf
        sh = jnp.where(valid, sh, NEG_INF)
        sink = sinks_ref[head]
        m = jnp.maximum(jnp.max(sh, axis=-1, keepdims=True), sink)
        p = jnp.exp(sh - m)
        l = jnp.sum(p, axis=-1, keepdims=True) + jnp.exp(sink - m)
        outs.append(jnp.dot(p.astype(BF16), vd, preferred_element_type=F32) / l)
    o_ref[...] = jnp.where(lo, outs[0], outs[1]).astype(o_ref.dtype)


def _swa_attention(proj, sinks, batch, seq, *, tq=256):
    t = proj.shape[0]
    tq = min(tq, seq)
    nq = seq // tq
    npairs = SWA_Q_HEADS // 2
    pairs_per_col = LANES // SWA_HEAD_DIM * (SWA_Q_HEADS // SWA_KV_HEADS) // 2
    kcol = SWA_Q_DIM // LANES
    vcol = (SWA_Q_DIM + SWA_KV_DIM) // LANES
    sub = tq // SWA_BLOCK

    def own(col):
        return lambda b, p, qi, *_: (b * nq + qi, col + p // pairs_per_col)

    def prev(col):
        return lambda b, p, qi, *_: (jnp.maximum((b * nq + qi) * sub - 1, 0), col + p // pairs_per_col)

    grid_spec = pltpu.PrefetchScalarGridSpec(
        num_scalar_prefetch=2,
        grid=(batch, npairs, nq),
        in_specs=[pl.BlockSpec((tq, LANES), lambda b, p, qi, *_: (b * nq + qi, p)),
                  pl.BlockSpec((tq, LANES), own(kcol)),
                  pl.BlockSpec((SWA_BLOCK, LANES), prev(kcol)),
                  pl.BlockSpec((tq, LANES), own(vcol)),
                  pl.BlockSpec((SWA_BLOCK, LANES), prev(vcol))],
        out_specs=pl.BlockSpec((tq, LANES), lambda b, p, qi, *_: (b * nq + qi, p)),
    )
    kern = functools.partial(_swa_attn_kernel, tq=tq)
    return pl.pallas_call(
        kern,
        grid_spec=grid_spec,
        out_shape=jax.ShapeDtypeStruct((t, SWA_Q_DIM), BF16),
        compiler_params=_params(("parallel", "parallel", "arbitrary")),
    )(sinks.astype(F32), _slopes(SWA_Q_HEADS), proj, proj, proj, proj, proj)


def _ffn_kernel(te_ref, nt_ref, x_ref, wg_ref, wu_ref, wd_ref, o_ref):
    t = pl.program_id(0)
    f = pl.program_id(1)
    live = t < nt_ref[0]

    @pl.when(f == 0)
    def _():
        o_ref[...] = jnp.zeros_like(o_ref)

    @pl.when(live)
    def _():
        x = x_ref[...]
        g = jnp.dot(x, wg_ref[0].astype(BF16), preferred_element_type=F32)
        u = jnp.dot(x, wu_ref[0].astype(BF16), preferred_element_type=F32)
        a = (g * _sigmoid(g) * u).astype(BF16)
        o_ref[...] += jnp.dot(a, wd_ref[0].astype(BF16), preferred_element_type=F32)


def _ffn(x, w_gate, w_up, w_down, tile_expert, n_tiles, *, tm, tf=256):
    p, d = x.shape
    n_exp, _, ff = w_gate.shape
    nf = ff // tf
    nt_max = p // tm

    def xmap(t, f, te, nt):
        return (jnp.minimum(t, nt[0] - 1), 0)

    def fcol(t, f, nt):
        return jnp.where(t < nt[0], f, nf - 1)

    grid_spec = pltpu.PrefetchScalarGridSpec(
        num_scalar_prefetch=2,
        grid=(nt_max, nf),
        in_specs=[pl.BlockSpec((tm, d), xmap),
                  pl.BlockSpec((1, d, tf), lambda t, f, te, nt: (te[t], 0, fcol(t, f, nt))),
                  pl.BlockSpec((1, d, tf), lambda t, f, te, nt: (te[t], 0, fcol(t, f, nt))),
                  pl.BlockSpec((1, tf, d), lambda t, f, te, nt: (te[t], fcol(t, f, nt), 0))],
        out_specs=pl.BlockSpec((tm, d), lambda t, f, te, nt: (t, 0)),
    )
    return pl.pallas_call(
        _ffn_kernel,
        grid_spec=grid_spec,
        out_shape=jax.ShapeDtypeStruct((p, d), F32),
        compiler_params=_params(("arbitrary", "arbitrary")),
    )(tile_expert, n_tiles, x, w_gate, w_up, w_down)


def _router_kernel(h_ref, r_ref, o_ref):
    def split(x):
        hi = x.astype(BF16)
        return hi, (x - hi.astype(F32)).astype(BF16)

    h_hi, h_lo = split(h_ref[...])
    r_hi, r_lo = split(r_ref[...])
    logits = (jnp.dot(h_hi, r_hi, preferred_element_type=F32)
              + jnp.dot(h_hi, r_lo, preferred_element_type=F32)
              + jnp.dot(h_lo, r_hi, preferred_element_type=F32))
    lane = lax.broadcasted_iota(jnp.int32, logits.shape, 1)
    l1 = jnp.where(lane < N_EXPERTS, logits, NEG_INF)
    m1 = jnp.max(l1, axis=-1, keepdims=True)
    i1 = jnp.min(jnp.where(l1 == m1, lane, LANES), axis=-1, keepdims=True)
    l2 = jnp.where(lane == i1, NEG_INF, l1)
    m2 = jnp.max(l2, axis=-1, keepdims=True)
    i2 = jnp.min(jnp.where(l2 == m2, lane, LANES), axis=-1, keepdims=True)
    e = jnp.exp(m2 - m1)
    g1 = 1.0 / (1.0 + e)
    g2 = e / (1.0 + e)
    out = jnp.where(lane == 0, i1.astype(F32),
                    jnp.where(lane == 1, i2.astype(F32),
                              jnp.where(lane == 2, g1, jnp.where(lane == 3, g2, 0.0))))
    o_ref[...] = out


def _router(h, router_padded, *, tm=512):
    m, d = h.shape
    tm = min(tm, m)
    return pl.pallas_call(
        _router_kernel,
        grid=(m // tm,),
        in_specs=[pl.BlockSpec((tm, d), lambda i: (i, 0)),
                  pl.BlockSpec((d, LANES), lambda i: (0, 0))],
        out_specs=pl.BlockSpec((tm, LANES), lambda i: (i, 0)),
        out_shape=jax.ShapeDtypeStruct((m, LANES), F32),
        compiler_params=_params(("parallel",)),
    )(h, router_padded)


def _row_copy(src_hbm, row, dst, slot, sem):
    return pltpu.make_async_copy(src_hbm.at[pl.ds(row, 1), :], dst.at[pl.ds(slot, 1), :], sem)


def _gather_kernel(src_ref, h_hbm, o_ref, buf, sem, *, tg):
    base = pl.program_id(0) * tg

    def issue(r, carry):
        _row_copy(h_hbm, src_ref[base + r], buf, r, sem).start()
        return carry

    def drain(r, carry):
        _row_copy(h_hbm, 0, buf, r, sem).wait()
        return carry

    lax.fori_loop(0, tg, issue, 0)
    lax.fori_loop(0, tg, drain, 0)
    o_ref[...] = buf[...].astype(o_ref.dtype)


def _gather_rows(h, src, *, tg=512):
    p = src.shape[0]
    d = h.shape[1]
    kern = functools.partial(_gather_kernel, tg=tg)
    grid_spec = pltpu.PrefetchScalarGridSpec(
        num_scalar_prefetch=1,
        grid=(p // tg,),
        in_specs=[pl.BlockSpec(memory_space=pl.ANY)],
        out_specs=pl.BlockSpec((tg, d), lambda i, s: (i, 0)),
        scratch_shapes=[pltpu.VMEM((tg, d), F32), pltpu.SemaphoreType.DMA(())],
    )
    return pl.pallas_call(
        kern,
        grid_spec=grid_spec,
        out_shape=jax.ShapeDtypeStruct((p, d), BF16),
        compiler_params=_params(("arbitrary",)),
    )(src, h)


def _combine_ln_kernel(pos_ref, y_hbm, route_ref, h_ref, g_ref, b_ref, hf_ref, hb_ref,
                       buf0, buf1, sem, *, tm):
    base = pl.program_id(0) * tm

    def issue(r, carry):
        _row_copy(y_hbm, pos_ref[2 * (base + r)], buf0, r, sem).start()
        _row_copy(y_hbm, pos_ref[2 * (base + r) + 1], buf1, r, sem).start()
        return carry

    def drain(r, carry):
        _row_copy(y_hbm, 0, buf0, r, sem).wait()
        _row_copy(y_hbm, 0, buf1, r, sem).wait()
        return carry

    lax.fori_loop(0, tm, issue, 0)
    lax.fori_loop(0, tm, drain, 0)
    route = route_ref[...]
    y = route[:, 2:3] * buf0[...] + route[:, 3:4] * buf1[...]
    out = _layer_norm_rows(ALPHA * h_ref[...] + y, g_ref[...], b_ref[...])
    hf_ref[...] = out
    hb_ref[...] = out.astype(BF16)


def _combine_ln(pos, ys, route, h, g, b, *, tm=256):
    m, d = h.shape
    tm = min(tm, m)
    kern = functools.partial(_combine_ln_kernel, tm=tm)
    row = pl.BlockSpec((tm, d), lambda i, s: (i, 0))
    vec = pl.BlockSpec((1, d), lambda i, s: (0, 0))
    grid_spec = pltpu.PrefetchScalarGridSpec(
        num_scalar_prefetch=1,
        grid=(m // tm,),
        in_specs=[pl.BlockSpec(memory_space=pl.ANY),
                  pl.BlockSpec((tm, LANES), lambda i, s: (i, 0)),
                  row, vec, vec],
        out_specs=[row, row],
        scratch_shapes=[pltpu.VMEM((tm, d), F32), pltpu.VMEM((tm, d), F32),
                        pltpu.SemaphoreType.DMA(())],
    )
    return pl.pallas_call(
        kern,
        grid_spec=grid_spec,
        out_shape=[jax.ShapeDtypeStruct((m, d), F32), jax.ShapeDtypeStruct((m, d), BF16)],
        compiler_params=_params(("arbitrary",)),
    )(pos, ys, route, h, g, b)


def _moe_layer(hf, hb, router, w_gate, w_up, w_down, layer, g, b, *, tm):
    t, d = hf.shape
    tm = min(tm, t)
    route = _router(hf, jnp.pad(router.astype(F32), ((0, 0), (0, LANES - N_EXPERTS))))

    expert = route[:, :TOP_K].astype(jnp.int32).reshape(-1)
    onehot = (expert[:, None] == jnp.arange(N_EXPERTS, dtype=jnp.int32)[None]).astype(jnp.int32)
    csum = jnp.cumsum(onehot, axis=0)
    counts = csum[-1]
    rank = jnp.take_along_axis(csum, expert[:, None], axis=1)[:, 0] - 1
    tiles_per = (counts + tm - 1) // tm
    tile_end = jnp.cumsum(tiles_per)
    start = (tile_end - tiles_per) * tm
    pos = (start[expert] + rank).astype(jnp.int32)
    nt_max = (t * TOP_K) // tm + N_EXPERTS
    n_tiles = tile_end[-1].astype(jnp.int32)
    src = jnp.zeros((nt_max * tm,), jnp.int32).at[pos].set(
        jnp.arange(t * TOP_K, dtype=jnp.int32) // TOP_K)
    tidx = jnp.arange(nt_max, dtype=jnp.int32)
    tile_expert = jnp.searchsorted(tile_end, jnp.minimum(tidx, n_tiles - 1), side="right")
    tile_expert = (jnp.minimum(tile_expert, N_EXPERTS - 1) + layer * N_EXPERTS).astype(jnp.int32)

    xs = _gather_rows(hf, src)
    ys = _ffn(xs, w_gate, w_up, w_down, tile_expert, n_tiles.reshape(1), tm=tm)
    return _combine_ln(pos, ys, route, hf, g, b)


def _dense_layer(hf, hb, w_gate, w_up, w_down, layer, g, b, *, tm):
    t = hf.shape[0]
    tm = min(tm, t)
    nt = t // tm
    ys = _ffn(hb, w_gate, w_up, w_down,
              jnp.full((nt,), layer, jnp.int32), jnp.full((1,), nt, jnp.int32), tm=tm)
    return _res_ln(ys, hf, g, b)


def _to_bf16_kernel(x_ref, o_ref):
    o_ref[...] = x_ref[...].astype(BF16)


def _to_bf16(x, *, tm=512):
    m, n = x.shape
    tm = min(tm, m)
    spec = pl.BlockSpec((tm, n), lambda i: (i, 0))
    return pl.pallas_call(
        _to_bf16_kernel, grid=(m // tm,), in_specs=[spec], out_specs=spec,
        out_shape=jax.ShapeDtypeStruct((m, n), BF16),
        compiler_params=_params(("parallel",)),
    )(x)


def _row(v):
    return v.astype(F32).reshape(1, -1)


def kernel(x, ln_gain, ln_bias, diff_w_in, diff_lambda, diff_subln_gain, diff_w_out,
           fox_w_in, fox_b_f, fox_qk_gain, fox_w_out, swa_w_in, swa_sinks, swa_w_out,
           dense_w_gate, dense_w_up, dense_w_down, moe_router, moe_w_gate, moe_w_up,
           moe_w_down, *, ffn_tm=1024):
    batch, seq, d = x.shape
    t = batch * seq
    hf = x.reshape(t, d).astype(F32)
    hb = _to_bf16(hf)
    depth = ln_gain.shape[0]
    moe_w = [w.reshape((-1,) + w.shape[2:]) for w in (moe_w_gate, moe_w_up, moe_w_down)]
    for i in range(depth):
        kind, j = i % 3, i // 3
        if kind == 0:
            proj = _proj(hb, diff_w_in, j, 3 * D_MODEL)
            o = _diff_attention(proj, diff_lambda[j], diff_subln_gain[j], batch, seq, i)
            w_out = diff_w_out
        elif kind == 1:
            w_in = fox_w_in[j]
            qkv = 3 * D_MODEL
            gain = jnp.concatenate([jnp.tile(fox_qk_gain[j, 0].astype(F32), FOX_HEADS),
                                    jnp.tile(fox_qk_gain[j, 1].astype(F32), FOX_HEADS),
                                    jnp.ones((D_MODEL,), F32)]).reshape(1, qkv)
            proj = _proj(hb, fox_w_in, j, qkv, gain=gain, n_norm=2 * D_MODEL)
            gate = _proj(hb, w_in[None, :, qkv + FOX_HEADS:], 0, D_MODEL, out_dtype=F32)
            wf_t = jnp.pad(w_in[:, qkv:qkv + FOX_HEADS].T, ((0, LANES - FOX_HEADS), (0, 0)))
            bf_col = jnp.pad(fox_b_f[j].astype(F32), (0, LANES - FOX_HEADS)).reshape(LANES, 1)
            ccol = _fox_cumlog(hb, wf_t, bf_col, batch, seq)
            o = _fox_attention(proj, ccol, gate, batch, seq)
            w_out = fox_w_out
        else:
            proj = _proj(hb, swa_w_in, j, SWA_Q_DIM + 2 * SWA_KV_DIM)
            o = _swa_attention(proj, swa_sinks[j], batch, seq)
            w_out = swa_w_out
        hf, hb = _outproj_ln(o, w_out, j, hf, _row(ln_gain[i, 0]), _row(ln_bias[i, 0]))
        c = i // 2
        g, b = _row(ln_gain[i, 1]), _row(ln_bias[i, 1])
        if i % 2 == 0:
            hf, hb = _dense_layer(hf, hb, dense_w_gate, dense_w_up, dense_w_down, c, g, b, tm=ffn_tm)
        else:
            hf, hb = _moe_layer(hf, hb, moe_router[c], *moe_w, c, g, b, tm=ffn_tm)
    return hf.reshape(batch, seq, d)
```

```python
import functools
import math

import numpy as np
import jax
import jax.numpy as jnp
from jax import lax
from jax.experimental import pallas as pl
from jax.experimental.pallas import tpu as pltpu

F32 = jnp.float32
BF16 = jnp.bfloat16

D_MODEL = 2048
DEPTH = 4
LN_EPS = 1e-5
ALPHA = (2 * DEPTH) ** 0.25

DIFF_HEADS = 16
DIFF_HEAD_DIM = 64
FOX_HEADS = 16
FOX_HEAD_DIM = 128
SWA_Q_HEADS = 32
SWA_KV_HEADS = 4
SWA_HEAD_DIM = 64
SWA_BLOCK = 128
SWA_Q_DIM = SWA_Q_HEADS * SWA_HEAD_DIM
SWA_KV_DIM = SWA_KV_HEADS * SWA_HEAD_DIM
N_EXPERTS = 8
TOP_K = 2

LANES = 128
VMEM_LIMIT = 56 * 1024 * 1024

NEG_INF = float("-inf")
_NT = (((1,), (1,)), ((), ()))


def _slopes(n):
    def pow2(m):
        start = 2.0 ** (-8.0 / m)
        return [start ** (i + 1) for i in range(m)]
    if n & (n - 1) == 0:
        s = pow2(n)
    else:
        c = 2 ** int(math.floor(math.log2(n)))
        s = pow2(c) + pow2(2 * c)[0::2][: n - c]
    return jnp.asarray(np.array(s, dtype=np.float32))


def _params(sem, vmem=VMEM_LIMIT):
    return pltpu.CompilerParams(dimension_semantics=sem, vmem_limit_bytes=vmem)


def _layer_norm_rows(y, g, b):
    mu = jnp.mean(y, axis=-1, keepdims=True)
    yc = y - mu
    var = jnp.mean(yc * yc, axis=-1, keepdims=True)
    return yc * lax.rsqrt(var + LN_EPS) * g + b


def _sigmoid(x):
    return 1.0 / (1.0 + jnp.exp(-x))


def _proj_kernel(a_ref, w_ref, g_ref, o_ref, *, n_norm_tiles, tn):
    acc = jnp.dot(a_ref[...], w_ref[...].astype(BF16), preferred_element_type=F32)
    if n_norm_tiles == 0:
        o_ref[...] = acc.astype(o_ref.dtype)
        return
    j = pl.program_id(1)

    @pl.when(j < n_norm_tiles)
    def _():
        for c in range(tn // LANES):
            sl = slice(c * LANES, (c + 1) * LANES)
            blk = acc[:, sl]
            ms = jnp.mean(blk * blk, axis=-1, keepdims=True)
            o_ref[:, sl] = (blk * lax.rsqrt(ms + LN_EPS) * g_ref[:, sl]).astype(o_ref.dtype)

    @pl.when(j >= n_norm_tiles)
    def _():
        o_ref[...] = acc.astype(o_ref.dtype)


def _proj(a, w, layer, n_out, *, gain=None, n_norm=0, out_dtype=BF16, tm=1024, tn=512):
    m, k = a.shape
    tm = min(tm, m)
    tn = min(tn, n_out)
    if gain is None:
        gain = jnp.ones((1, n_out), F32)
    kern = functools.partial(_proj_kernel, n_norm_tiles=n_norm // tn, tn=tn)
    return pl.pallas_call(
        kern,
        grid=(m // tm, n_out // tn),
        in_specs=[pl.BlockSpec((tm, k), lambda i, j: (i, 0)),
                  pl.BlockSpec((None, k, tn), lambda i, j: (layer, 0, j)),
                  pl.BlockSpec((1, tn), lambda i, j: (0, j))],
        out_specs=pl.BlockSpec((tm, tn), lambda i, j: (i, j)),
        out_shape=jax.ShapeDtypeStruct((m, n_out), out_dtype),
        compiler_params=_params(("parallel", "arbitrary")),
    )(a, w, gain)


def _outproj_ln_kernel(o_ref, w_ref, h_ref, g_ref, b_ref, hf_ref, hb_ref, acc_ref):
    k = pl.program_id(1)

    @pl.when(k == 0)
    def _():
        acc_ref[...] = jnp.zeros_like(acc_ref)

    acc_ref[...] += jnp.dot(o_ref[...], w_ref[...].astype(BF16), preferred_element_type=F32)

    @pl.when(k == pl.num_programs(1) - 1)
    def _():
        out = _layer_norm_rows(ALPHA * h_ref[...] + acc_ref[...], g_ref[...], b_ref[...])
        hf_ref[...] = out
        hb_ref[...] = out.astype(BF16)


def _outproj_ln(o, w, layer, h, g, b, *, tm=512, tk=1024):
    m, k = o.shape
    n = w.shape[2]
    tm = min(tm, m)
    return pl.pallas_call(
        _outproj_ln_kernel,
        grid=(m // tm, k // tk),
        in_specs=[pl.BlockSpec((tm, tk), lambda i, kk: (i, kk)),
                  pl.BlockSpec((None, tk, n), lambda i, kk: (layer, kk, 0)),
                  pl.BlockSpec((tm, n), lambda i, kk: (i, 0)),
                  pl.BlockSpec((1, n), lambda i, kk: (0, 0)),
                  pl.BlockSpec((1, n), lambda i, kk: (0, 0))],
        out_specs=[pl.BlockSpec((tm, n), lambda i, kk: (i, 0)),
                   pl.BlockSpec((tm, n), lambda i, kk: (i, 0))],
        out_shape=[jax.ShapeDtypeStruct((m, n), F32), jax.ShapeDtypeStruct((m, n), BF16)],
        scratch_shapes=[pltpu.VMEM((tm, n), F32)],
        compiler_params=_params(("parallel", "arbitrary")),
    )(o, w, h, g, b)


def _res_ln_kernel(y_ref, h_ref, g_ref, b_ref, hf_ref, hb_ref):
    out = _layer_norm_rows(ALPHA * h_ref[...] + y_ref[...], g_ref[...], b_ref[...])
    hf_ref[...] = out
    hb_ref[...] = out.astype(BF16)


def _res_ln(y, h, g, b, *, tm=512):
    m, n = h.shape
    tm = min(tm, m)
    row = pl.BlockSpec((tm, n), lambda i: (i, 0))
    vec = pl.BlockSpec((1, n), lambda i: (0, 0))
    return pl.pallas_call(
        _res_ln_kernel,
        grid=(m // tm,),
        in_specs=[row, row, vec, vec],
        out_specs=[row, row],
        out_shape=[jax.ShapeDtypeStruct((m, n), F32), jax.ShapeDtypeStruct((m, n), BF16)],
        compiler_params=_params(("parallel",)),
    )(y, h, g, b)


ATTN_TILE = 512


def _split3(x):
    hi = x.astype(BF16).astype(F32)
    r = x - hi
    mid = r.astype(BF16).astype(F32)
    lo = (r - mid).astype(BF16).astype(F32)
    return hi, mid, lo


def _lane_groups(lane, groups):
    out = jnp.zeros(lane.shape, F32)
    for g, parts in enumerate(groups):
        for i, part in enumerate(parts):
            out = jnp.where(lane == 3 * g + i, part, out)
    return out


def _flash_schedule(qi, scores, absorb, sa_ref, sb_ref):
    scores(0, sa_ref)

    def pair(j, carry):
        scores(2 * j + 1, sb_ref)
        absorb(2 * j, sa_ref, False)
        scores(2 * j + 2, sa_ref)
        absorb(2 * j + 1, sb_ref, False)
        return carry

    lax.fori_loop(0, qi // 2, pair, 0)

    @pl.when(qi % 2 == 1)
    def _():
        scores(qi, sb_ref)
        absorb(qi - 1, sa_ref, False)
        absorb(qi, sb_ref, True)

    @pl.when(qi % 2 == 0)
    def _():
        absorb(qi, sa_ref, True)


def _flash_absorb_t(st, v, m_ref, l_ref, acc_ref, c2):
    m = m_ref[...]
    m_new = jnp.maximum(m, jnp.max(st, axis=0, keepdims=True))
    alpha = jnp.exp2((m - m_new) * c2)
    p = jnp.exp2((st - m_new) * c2)
    l_ref[...] = alpha * l_ref[...] + jnp.sum(p, axis=0, keepdims=True)
    pv = lax.dot_general(v, p.astype(BF16), (((0,), (0,)), ((), ())), preferred_element_type=F32)
    acc_ref[...] = alpha * acc_ref[...] + pv
    m_ref[...] = m_new


def _flash_scratch_t(tk, cols):
    return [pltpu.VMEM((tk, cols), F32), pltpu.VMEM((tk, cols), F32),
            pltpu.VMEM((1, cols), F32), pltpu.VMEM((1, cols), F32),
            pltpu.VMEM((LANES, cols), F32)]


def _diff_attn_kernel(slopes_ref, q_ref, k_ref, v_ref, lam_ref, gain_ref, o_ref,
                      sa_ref, sb_ref, m_ref, l_ref, acc_ref, *, tq, lam_init):
    h = pl.program_id(1)
    qi = pl.program_id(2)
    half = DIFF_HEAD_DIM

    lane = lax.broadcasted_iota(jnp.int32, (tq, LANES), 1)
    row = lax.broadcasted_iota(jnp.int32, (tq, LANES), 0)
    slope = jnp.full((tq, LANES), slopes_ref[h], F32)
    sl = _split3(slope)
    neg_rows = _split3(-(slope * row.astype(F32)))
    q_bias = _lane_groups(lane, [sl, tuple(256.0 * x for x in sl), tuple(-x for x in sl), neg_rows])
    ones = (1.0, 1.0, 1.0)
    j_lo = (row & 255).astype(F32)
    j_hi = (row >> 8).astype(F32)
    k_bias_const = _lane_groups(lane, [(j_lo,) * 3, (j_hi,) * 3, (0.0,) * 3, ones])
    off_lanes = (lane >= 6) & (lane < 9)

    q = q_ref[...] * jnp.asarray(DIFF_HEAD_DIM ** -0.5, BF16)
    zero = jnp.zeros_like(q)
    qb = q_bias.astype(BF16)
    qs = jnp.concatenate(
        [jnp.concatenate([jnp.where(lane < half, q, zero), qb], axis=1),
         jnp.concatenate([jnp.where(lane < half, zero, q), qb], axis=1)], axis=0)

    krow = lax.broadcasted_iota(jnp.int32, (tq, tq), 0)
    qcol = lax.broadcasted_iota(jnp.int32, (tq, tq), 1)
    causal = jnp.concatenate([qcol >= krow, qcol >= krow], axis=1)

    m_ref[...] = jnp.full(m_ref.shape, NEG_INF, F32)
    l_ref[...] = jnp.zeros(l_ref.shape, F32)
    acc_ref[...] = jnp.zeros(acc_ref.shape, F32)

    def scores(kb, s_ref):
        start = pl.multiple_of(kb * tq, tq)
        off = ((qi - kb) * tq).astype(F32)
        kb_bias = jnp.where(off_lanes, off, k_bias_const).astype(BF16)
        k = jnp.concatenate([k_ref[pl.ds(start, tq), :], kb_bias], axis=1)
        s_ref[...] = lax.dot_general(k, qs, _NT, preferred_element_type=F32)

    def absorb(kb, s_ref, diag):
        s = s_ref[...]
        if diag:
            s = jnp.where(causal, s, NEG_INF)
        v = v_ref[pl.ds(pl.multiple_of(kb * tq, tq), tq), :]
        _flash_absorb_t(s, v, m_ref, l_ref, acc_ref, math.log2(math.e))

    _flash_schedule(qi, scores, absorb, sa_ref, sb_ref)

    lp = lam_ref[...]
    lam = (jnp.exp(jnp.sum(lp[0:1] * lp[1:2], axis=-1, keepdims=True))
           - jnp.exp(jnp.sum(lp[2:3] * lp[3:4], axis=-1, keepdims=True)) + lam_init)
    o = acc_ref[...] / l_ref[...]
    a = o[:, :tq] - lam * o[:, tq:]
    ms = jnp.mean(a * a, axis=0, keepdims=True)
    y = a * lax.rsqrt(ms + LN_EPS) * gain_ref[...] * (1.0 - lam_init)
    o_ref[...] = y.T.astype(o_ref.dtype)


def _diff_attention(proj, lam_params, gain, batch, seq, layer_idx, *, tq=ATTN_TILE):
    t = proj.shape[0]
    tq = min(tq, seq)
    nq = seq // tq
    nh = DIFF_HEADS
    lam_init = 0.8 - 0.6 * math.exp(-0.3 * layer_idx)
    kern = functools.partial(_diff_attn_kernel, tq=tq, lam_init=lam_init)
    grid_spec = pltpu.PrefetchScalarGridSpec(
        num_scalar_prefetch=1,
        grid=(batch, nh, nq),
        in_specs=[pl.BlockSpec((tq, LANES), lambda b, h, qi, s: (b * nq + qi, h)),
                  pl.BlockSpec((seq, LANES), lambda b, h, qi, s: (b, nh + h)),
                  pl.BlockSpec((seq, LANES), lambda b, h, qi, s: (b, 2 * nh + h)),
                  pl.BlockSpec((4, DIFF_HEAD_DIM), lambda b, h, qi, s: (0, 0)),
                  pl.BlockSpec((LANES, 1), lambda b, h, qi, s: (0, 0))],
        out_specs=pl.BlockSpec((tq, LANES), lambda b, h, qi, s: (b * nq + qi, h)),
        scratch_shapes=_flash_scratch_t(tq, 2 * tq),
    )
    return pl.pallas_call(
        kern,
        grid_spec=grid_spec,
        out_shape=jax.ShapeDtypeStruct((t, D_MODEL), BF16),
        compiler_params=_params(("parallel", "parallel", "arbitrary")),
    )(_slopes(nh), proj, proj, proj, lam_params.astype(F32), gain.astype(F32).reshape(LANES, 1))


def _fox_c_kernel(h_ref, wf_ref, bf_ref, ccol_ref, carry_ref, *, tb):
    j = pl.program_id(1)

    @pl.when(j == 0)
    def _():
        carry_ref[...] = jnp.zeros_like(carry_ref)

    x = lax.dot_general(wf_ref[...].astype(BF16), h_ref[...], _NT, preferred_element_type=F32)
    x = x + bf_ref[...]
    lf = jnp.minimum(x, 0.0) - jnp.log1p(jnp.exp(-jnp.abs(x)))
    ii = lax.broadcasted_iota(jnp.int32, (tb, tb), 0)
    jj = lax.broadcasted_iota(jnp.int32, (tb, tb), 1)
    tri = jnp.where(ii <= jj, 1.0, 0.0).astype(BF16)
    pre = sum(jnp.dot(part.astype(BF16), tri, preferred_element_type=F32) for part in _split3(lf))
    c = pre + carry_ref[...]
    ccol_ref[...] = c.T
    carry_ref[...] = c[:, tb - 1:tb]


def _fox_cumlog(hb, wf_t, bf_col, batch, seq, *, tb=512):
    t, d = hb.shape
    tb = min(tb, seq)
    nb = seq // tb
    kern = functools.partial(_fox_c_kernel, tb=tb)
    return pl.pallas_call(
        kern,
        grid=(batch, nb),
        in_specs=[pl.BlockSpec((tb, d), lambda b, j: (b * nb + j, 0)),
                  pl.BlockSpec((LANES, d), lambda b, j: (0, 0)),
                  pl.BlockSpec((LANES, 1), lambda b, j: (0, 0))],
        out_specs=pl.BlockSpec((tb, LANES), lambda b, j: (b * nb + j, 0)),
        out_shape=jax.ShapeDtypeStruct((t, LANES), F32),
        scratch_shapes=[pltpu.VMEM((LANES, 1), F32)],
        compiler_params=_params(("parallel", "arbitrary")),
    )(hb, wf_t, bf_col)


def _fox_attn_kernel(q_ref, k_ref, v_ref, cq_ref, ck_ref, g_ref, o_ref,
                     sa_ref, sb_ref, m_ref, l_ref, acc_ref, *, tq):
    h = pl.program_id(1)
    qi = pl.program_id(2)
    scale = FOX_HEAD_DIM ** -0.5
    c2 = scale * math.log2(math.e)

    lane = lax.broadcasted_iota(jnp.int32, (tq, LANES), 1)
    head = lane == h

    def head_col(x):
        return jnp.sum(jnp.where(head, x, 0.0), axis=1, keepdims=True)

    ci = head_col(cq_ref[...])
    cref = ci[0:1, :]
    ones = (1.0, 1.0, 1.0)
    q_bias = _lane_groups(lane, [_split3(jnp.broadcast_to((ci - cref) * (1.0 / scale), (tq, LANES))), ones])
    qa = jnp.concatenate([q_ref[...], q_bias.astype(BF16)], axis=1)

    krow = lax.broadcasted_iota(jnp.int32, (tq, tq), 0)
    qcol = lax.broadcasted_iota(jnp.int32, (tq, tq), 1)
    causal = qcol >= krow

    m_ref[...] = jnp.full(m_ref.shape, NEG_INF, F32)
    l_ref[...] = jnp.zeros(l_ref.shape, F32)
    acc_ref[...] = jnp.zeros(acc_ref.shape, F32)

    def scores(kb, s_ref):
        start = pl.multiple_of(kb * tq, tq)
        cj = head_col(ck_ref[pl.ds(start, tq), :])
        neg = jnp.broadcast_to((cref - cj) * (1.0 / scale), (tq, LANES))
        k_bias = _lane_groups(lane, [ones, _split3(neg)])
        k = jnp.concatenate([k_ref[pl.ds(start, tq), :], k_bias.astype(BF16)], axis=1)
        s_ref[...] = lax.dot_general(k, qa, _NT, preferred_element_type=F32)

    def absorb(kb, s_ref, diag):
        s = s_ref[...]
        if diag:
            s = jnp.where(causal, s, NEG_INF)
        v = v_ref[pl.ds(pl.multiple_of(kb * tq, tq), tq), :]
        _flash_absorb_t(s, v, m_ref, l_ref, acc_ref, c2)

    _flash_schedule(qi, scores, absorb, sa_ref, sb_ref)
    o_ref[...] = ((acc_ref[...] / l_ref[...]).T * _sigmoid(g_ref[...])).astype(o_ref.dtype)


def _fox_attention(proj, ccol, gate, batch, seq, *, tq=ATTN_TILE):
    t = proj.shape[0]
    tq = min(tq, seq)
    nq = seq // tq
    nh = FOX_HEADS
    kern = functools.partial(_fox_attn_kernel, tq=tq)
    return pl.pallas_call(
        kern,
        grid=(batch, nh, nq),
        in_specs=[pl.BlockSpec((tq, LANES), lambda b, h, qi: (b * nq + qi, h)),
                  pl.BlockSpec((seq, LANES), lambda b, h, qi: (b, nh + h)),
                  pl.BlockSpec((seq, LANES), lambda b, h, qi: (b, 2 * nh + h)),
                  pl.BlockSpec((tq, LANES), lambda b, h, qi: (b * nq + qi, 0)),
                  pl.BlockSpec((seq, LANES), lambda b, h, qi: (b, 0)),
                  pl.BlockSpec((tq, LANES), lambda b, h, qi: (b * nq + qi, h))],
        out_specs=pl.BlockSpec((tq, LANES), lambda b, h, qi: (b * nq + qi, h)),
        out_shape=jax.ShapeDtypeStruct((t, D_MODEL), BF16),
        scratch_shapes=_flash_scratch_t(tq, tq),
        compiler_params=_params(("parallel", "parallel", "arbitrary")),
    )(proj, proj, proj, ccol, ccol, gate)


def _swa_attn_kernel(sinks_ref, slopes_ref, q_ref, ko_ref, kp_ref, vo_ref, vp_ref, o_ref, *, tq):
    pair = pl.program_id(1)
    qi = pl.program_id(2)
    half = SWA_HEAD_DIM
    hk = pair // (SWA_Q_HEADS // SWA_KV_HEADS // 2)
    par = hk % 2
    tk = tq + SWA_BLOCK

    lane_k = lax.broadcasted_iota(jnp.int32, (tk, LANES), 1)
    keep = jnp.where(lane_k >= half, 1, 0) == par

    def dup(prev_ref, own_ref):
        x = jnp.concatenate([prev_ref[...], own_ref[...]], axis=0).astype(F32)
        return jnp.where(keep, x, pltpu.roll(x, half, 1)).astype(BF16)

    kd = dup(kp_ref, ko_ref)
    vd = dup(vp_ref, vo_ref)

    q = q_ref[...] * jnp.asarray(SWA_HEAD_DIM ** -0.5, BF16)
    lane = lax.broadcasted_iota(jnp.int32, (tq, LANES), 1)
    lo = lane < half
    zero = jnp.zeros_like(q)
    qs = jnp.concatenate([jnp.where(lo, q, zero), jnp.where(lo, zero, q)], axis=0)
    s = lax.dot_general(qs, kd, _NT, preferred_element_type=F32)

    r = lax.broadcasted_iota(jnp.int32, (tq, tk), 0)
    c = lax.broadcasted_iota(jnp.int32, (tq, tk), 1)
    dist = r - c + SWA_BLOCK
    valid = (dist >= 0) & (dist < SWA_BLOCK) & (qi * tq + r - dist >= 0)
    distf = dist.astype(F32)

    outs = []
    for hh in range(2):
        head = 2 * pair + hh
        sh = s[hh * tq:(hh + 1) * tq] - slopes_ref[head] * distf
        sh = jnp.where(valid, sh, NEG_INF)
        sink = sinks_ref[head]
        m = jnp.maximum(jnp.max(sh, axis=-1, keepdims=True), sink)
        p = jnp.exp(sh - m)
        l = jnp.sum(p, axis=-1, keepdims=True) + jnp.exp(sink - m)
        outs.append(jnp.dot(p.astype(BF16), vd, preferred_element_type=F32) / l)
    o_ref[...] = jnp.where(lo, outs[0], outs[1]).astype(o_ref.dtype)


def _swa_attention(proj, sinks, batch, seq, *, tq=256):
    t = proj.shape[0]
    tq = min(tq, seq)
    nq = seq // tq
    npairs = SWA_Q_HEADS // 2
    pairs_per_col = LANES // SWA_HEAD_DIM * (SWA_Q_HEADS // SWA_KV_HEADS) // 2
    kcol = SWA_Q_DIM // LANES
    vcol = (SWA_Q_DIM + SWA_KV_DIM) // LANES
    sub = tq // SWA_BLOCK

    def own(col):
        return lambda b, p, qi, *_: (b * nq + qi, col + p // pairs_per_col)

    def prev(col):
        return lambda b, p, qi, *_: (jnp.maximum((b * nq + qi) * sub - 1, 0), col + p // pairs_per_col)

    grid_spec = pltpu.PrefetchScalarGridSpec(
        num_scalar_prefetch=2,
        grid=(batch, npairs, nq),
        in_specs=[pl.BlockSpec((tq, LANES), lambda b, p, qi, *_: (b * nq + qi, p)),
                  pl.BlockSpec((tq, LANES), own(kcol)),
                  pl.BlockSpec((SWA_BLOCK, LANES), prev(kcol)),
                  pl.BlockSpec((tq, LANES), own(vcol)),
                  pl.BlockSpec((SWA_BLOCK, LANES), prev(vcol))],
        out_specs=pl.BlockSpec((tq, LANES), lambda b, p, qi, *_: (b * nq + qi, p)),
    )
    kern = functools.partial(_swa_attn_kernel, tq=tq)
    return pl.pallas_call(
        kern,
        grid_spec=grid_spec,
        out_shape=jax.ShapeDtypeStruct((t, SWA_Q_DIM), BF16),
        compiler_params=_params(("parallel", "parallel", "arbitrary")),
    )(sinks.astype(F32), _slopes(SWA_Q_HEADS), proj, proj, proj, proj, proj)


def _ffn_kernel(te_ref, nt_ref, x_ref, wg_ref, wu_ref, wd_ref, o_ref):
    t = pl.program_id(0)
    f = pl.program_id(1)
    live = t < nt_ref[0]

    @pl.when(f == 0)
    def _():
        o_ref[...] = jnp.zeros_like(o_ref)

    @pl.when(live)
    def _():
        x = x_ref[...]
        g = jnp.dot(x, wg_ref[0].astype(BF16), preferred_element_type=F32)
        u = jnp.dot(x, wu_ref[0].astype(BF16), preferred_element_type=F32)
        a = (g * _sigmoid(g) * u).astype(BF16)
        o_ref[...] += jnp.dot(a, wd_ref[0].astype(BF16), preferred_element_type=F32)


def _ffn(x, w_gate, w_up, w_down, tile_expert, n_tiles, *, tm, tf=256):
    p, d = x.shape
    n_exp, _, ff = w_gate.shape
    nf = ff // tf
    nt_max = p // tm

    def xmap(t, f, te, nt):
        return (jnp.minimum(t, nt[0] - 1), 0)

    def fcol(t, f, nt):
        return jnp.where(t < nt[0], f, nf - 1)

    grid_spec = pltpu.PrefetchScalarGridSpec(
        num_scalar_prefetch=2,
        grid=(nt_max, nf),
        in_specs=[pl.BlockSpec((tm, d), xmap),
                  pl.BlockSpec((1, d, tf), lambda t, f, te, nt: (te[t], 0, fcol(t, f, nt))),
                  pl.BlockSpec((1, d, tf), lambda t, f, te, nt: (te[t], 0, fcol(t, f, nt))),
                  pl.BlockSpec((1, tf, d), lambda t, f, te, nt: (te[t], fcol(t, f, nt), 0))],
        out_specs=pl.BlockSpec((tm, d), lambda t, f, te, nt: (t, 0)),
    )
    return pl.pallas_call(
        _ffn_kernel,
        grid_spec=grid_spec,
        out_shape=jax.ShapeDtypeStruct((p, d), F32),
        compiler_params=_params(("arbitrary", "arbitrary")),
    )(tile_expert, n_tiles, x, w_gate, w_up, w_down)


def _router_kernel(h_ref, r_ref, o_ref):
    def split(x):
        hi = x.astype(BF16)
        return hi, (x - hi.astype(F32)).astype(BF16)

    h_hi, h_lo = split(h_ref[...])
    r_hi, r_lo = split(r_ref[...])
    logits = (jnp.dot(h_hi, r_hi, preferred_element_type=F32)
              + jnp.dot(h_hi, r_lo, preferred_element_type=F32)
              + jnp.dot(h_lo, r_hi, preferred_element_type=F32))
    lane = lax.broadcasted_iota(jnp.int32, logits.shape, 1)
    l1 = jnp.where(lane < N_EXPERTS, logits, NEG_INF)
    m1 = jnp.max(l1, axis=-1, keepdims=True)
    i1 = jnp.min(jnp.where(l1 == m1, lane, LANES), axis=-1, keepdims=True)
    l2 = jnp.where(lane == i1, NEG_INF, l1)
    m2 = jnp.max(l2, axis=-1, keepdims=True)
    i2 = jnp.min(jnp.where(l2 == m2, lane, LANES), axis=-1, keepdims=True)
    e = jnp.exp(m2 - m1)
    g1 = 1.0 / (1.0 + e)
    g2 = e / (1.0 + e)
    out = jnp.where(lane == 0, i1.astype(F32),
                    jnp.where(lane == 1, i2.astype(F32),
                              jnp.where(lane == 2, g1, jnp.where(lane == 3, g2, 0.0))))
    o_ref[...] = out


def _router(h, router_padded, *, tm=512):
    m, d = h.shape
    tm = min(tm, m)
    return pl.pallas_call(
        _router_kernel,
        grid=(m // tm,),
        in_specs=[pl.BlockSpec((tm, d), lambda i: (i, 0)),
                  pl.BlockSpec((d, LANES), lambda i: (0, 0))],
        out_specs=pl.BlockSpec((tm, LANES), lambda i: (i, 0)),
        out_shape=jax.ShapeDtypeStruct((m, LANES), F32),
        compiler_params=_params(("parallel",)),
    )(h, router_padded)


def _row_copy(src_hbm, row, dst, slot, sem):
    return pltpu.make_async_copy(src_hbm.at[pl.ds(row, 1), :], dst.at[pl.ds(slot, 1), :], sem)


def _double_buffered(issue, drain):
    i = pl.program_id(0)

    @pl.when(i == 0)
    def _():
        issue(0, 0)

    @pl.when(i + 1 < pl.num_programs(0))
    def _():
        issue(i + 1, (i + 1) % 2)

    drain(i % 2)
    return i % 2


def _gather_kernel(src_ref, h_hbm, o_ref, buf, sem, *, tg):
    def issue(tile, slot):
        def body(r, carry):
            _row_copy(h_hbm, src_ref[tile * tg + r], buf.at[slot], r, sem.at[slot]).start()
            return carry
        lax.fori_loop(0, tg, body, 0, unroll=8)

    def drain(slot):
        def body(r, carry):
            _row_copy(h_hbm, 0, buf.at[slot], r, sem.at[slot]).wait()
            return carry
        lax.fori_loop(0, tg, body, 0, unroll=8)

    slot = _double_buffered(issue, drain)
    o_ref[...] = buf[slot].astype(o_ref.dtype)


def _gather_rows(h, src, *, tg=512):
    p = src.shape[0]
    d = h.shape[1]
    kern = functools.partial(_gather_kernel, tg=tg)
    grid_spec = pltpu.PrefetchScalarGridSpec(
        num_scalar_prefetch=1,
        grid=(p // tg,),
        in_specs=[pl.BlockSpec(memory_space=pl.ANY)],
        out_specs=pl.BlockSpec((tg, d), lambda i, s: (i, 0)),
        scratch_shapes=[pltpu.VMEM((2, tg, d), F32), pltpu.SemaphoreType.DMA((2,))],
    )
    return pl.pallas_call(
        kern,
        grid_spec=grid_spec,
        out_shape=jax.ShapeDtypeStruct((p, d), BF16),
        compiler_params=_params(("arbitrary",)),
    )(src, h)


def _combine_ln_kernel(pos_ref, y_hbm, route_ref, h_ref, g_ref, b_ref, hf_ref, hb_ref,
                       buf0, buf1, sem, *, tm):
    def issue(tile, slot):
        def body(r, carry):
            tok = 2 * (tile * tm + r)
            _row_copy(y_hbm, pos_ref[tok], buf0.at[slot], r, sem.at[slot]).start()
            _row_copy(y_hbm, pos_ref[tok + 1], buf1.at[slot], r, sem.at[slot]).start()
            return carry
        lax.fori_loop(0, tm, body, 0, unroll=4)

    def drain(slot):
        def body(r, carry):
            _row_copy(y_hbm, 0, buf0.at[slot], r, sem.at[slot]).wait()
            _row_copy(y_hbm, 0, buf1.at[slot], r, sem.at[slot]).wait()
            return carry
        lax.fori_loop(0, tm, body, 0, unroll=4)

    slot = _double_buffered(issue, drain)
    route = route_ref[...]
    y = route[:, 2:3] * buf0[slot] + route[:, 3:4] * buf1[slot]
    out = _layer_norm_rows(ALPHA * h_ref[...] + y, g_ref[...], b_ref[...])
    hf_ref[...] = out
    hb_ref[...] = out.astype(BF16)


def _combine_ln(pos, ys, route, h, g, b, *, tm=256):
    m, d = h.shape
    tm = min(tm, m)
    kern = functools.partial(_combine_ln_kernel, tm=tm)
    row = pl.BlockSpec((tm, d), lambda i, s: (i, 0))
    vec = pl.BlockSpec((1, d), lambda i, s: (0, 0))
    grid_spec = pltpu.PrefetchScalarGridSpec(
        num_scalar_prefetch=1,
        grid=(m // tm,),
        in_specs=[pl.BlockSpec(memory_space=pl.ANY),
                  pl.BlockSpec((tm, LANES), lambda i, s: (i, 0)),
                  row, vec, vec],
        out_specs=[row, row],
        scratch_shapes=[pltpu.VMEM((2, tm, d), F32), pltpu.VMEM((2, tm, d), F32),
                        pltpu.SemaphoreType.DMA((2,))],
    )
    return pl.pallas_call(
        kern,
        grid_spec=grid_spec,
        out_shape=[jax.ShapeDtypeStruct((m, d), F32), jax.ShapeDtypeStruct((m, d), BF16)],
        compiler_params=_params(("arbitrary",)),
    )(pos, ys, route, h, g, b)


def _moe_layer(hf, hb, router, w_gate, w_up, w_down, layer, g, b, *, tm):
    t, d = hf.shape
    tm = min(tm, t)
    route = _router(hf, jnp.pad(router.astype(F32), ((0, 0), (0, LANES - N_EXPERTS))))

    expert = route[:, :TOP_K].astype(jnp.int32).reshape(-1)
    onehot = (expert[:, None] == jnp.arange(N_EXPERTS, dtype=jnp.int32)[None]).astype(jnp.int32)
    csum = jnp.cumsum(onehot, axis=0)
    counts = csum[-1]
    rank = jnp.take_along_axis(csum, expert[:, None], axis=1)[:, 0] - 1
    tiles_per = (counts + tm - 1) // tm
    tile_end = jnp.cumsum(tiles_per)
    start = (tile_end - tiles_per) * tm
    pos = (start[expert] + rank).astype(jnp.int32)
    nt_max = (t * TOP_K) // tm + N_EXPERTS
    n_tiles = tile_end[-1].astype(jnp.int32)
    src = jnp.zeros((nt_max * tm,), jnp.int32).at[pos].set(
        jnp.arange(t * TOP_K, dtype=jnp.int32) // TOP_K)
    tidx = jnp.arange(nt_max, dtype=jnp.int32)
    tile_expert = jnp.searchsorted(tile_end, jnp.minimum(tidx, n_tiles - 1), side="right")
    tile_expert = (jnp.minimum(tile_expert, N_EXPERTS - 1) + layer * N_EXPERTS).astype(jnp.int32)

    xs = _gather_rows(hf, src)
    ys = _ffn(xs, w_gate, w_up, w_down, tile_expert, n_tiles.reshape(1), tm=tm)
    return _combine_ln(pos, ys, route, hf, g, b)


def _dense_layer(hf, hb, w_gate, w_up, w_down, layer, g, b, *, tm):
    t = hf.shape[0]
    tm = min(tm, t)
    nt = t // tm
    ys = _ffn(hb, w_gate, w_up, w_down,
              jnp.full((nt,), layer, jnp.int32), jnp.full((1,), nt, jnp.int32), tm=tm)
    return _res_ln(ys, hf, g, b)


def _to_bf16_kernel(x_ref, o_ref):
    o_ref[...] = x_ref[...].astype(BF16)


def _to_bf16(x, *, tm=512):
    m, n = x.shape
    tm = min(tm, m)
    spec = pl.BlockSpec((tm, n), lambda i: (i, 0))
    return pl.pallas_call(
        _to_bf16_kernel, grid=(m // tm,), in_specs=[spec], out_specs=spec,
        out_shape=jax.ShapeDtypeStruct((m, n), BF16),
        compiler_params=_params(("parallel",)),
    )(x)


def _row(v):
    return v.astype(F32).reshape(1, -1)


def kernel(x, ln_gain, ln_bias, diff_w_in, diff_lambda, diff_subln_gain, diff_w_out,
           fox_w_in, fox_b_f, fox_qk_gain, fox_w_out, swa_w_in, swa_sinks, swa_w_out,
           dense_w_gate, dense_w_up, dense_w_down, moe_router, moe_w_gate, moe_w_up,
           moe_w_down, *, ffn_tm=1024):
    batch, seq, d = x.shape
    t = batch * seq
    hf = x.reshape(t, d).astype(F32)
    hb = _to_bf16(hf)
    depth = ln_gain.shape[0]
    moe_w = [w.reshape((-1,) + w.shape[2:]) for w in (moe_w_gate, moe_w_up, moe_w_down)]
    for i in range(depth):
        kind, j = i % 3, i // 3
        if kind == 0:
            proj = _proj(hb, diff_w_in, j, 3 * D_MODEL)
            o = _diff_attention(proj, diff_lambda[j], diff_subln_gain[j], batch, seq, i)
            w_out = diff_w_out
        elif kind == 1:
            w_in = fox_w_in[j]
            qkv = 3 * D_MODEL
            gain = jnp.concatenate([jnp.tile(fox_qk_gain[j, 0].astype(F32), FOX_HEADS),
                                    jnp.tile(fox_qk_gain[j, 1].astype(F32), FOX_HEADS),
                                    jnp.ones((D_MODEL,), F32)]).reshape(1, qkv)
            proj = _proj(hb, fox_w_in, j, qkv, gain=gain, n_norm=2 * D_MODEL)
            gate = _proj(hb, w_in[None, :, qkv + FOX_HEADS:], 0, D_MODEL, out_dtype=F32)
            wf_t = jnp.pad(w_in[:, qkv:qkv + FOX_HEADS].T, ((0, LANES - FOX_HEADS), (0, 0)))
            bf_col = jnp.pad(fox_b_f[j].astype(F32), (0, LANES - FOX_HEADS)).reshape(LANES, 1)
            ccol = _fox_cumlog(hb, wf_t, bf_col, batch, seq)
            o = _fox_attention(proj, ccol, gate, batch, seq)
            w_out = fox_w_out
        else:
            proj = _proj(hb, swa_w_in, j, SWA_Q_DIM + 2 * SWA_KV_DIM)
            o = _swa_attention(proj, swa_sinks[j], batch, seq)
            w_out = swa_w_out
        hf, hb = _outproj_ln(o, w_out, j, hf, _row(ln_gain[i, 0]), _row(ln_bias[i, 0]))
        c = i // 2
        g, b = _row(ln_gain[i, 1]), _row(ln_bias[i, 1])
        if i % 2 == 0:
            hf, hb = _dense_layer(hf, hb, dense_w_gate, dense_w_up, dense_w_down, c, g, b, tm=ffn_tm)
        else:
            hf, hb = _moe_layer(hf, hb, moe_router[c], *moe_w, c, g, b, tm=ffn_tm)
    return hf.reshape(batch, seq, d)
```

```python
import functools
import math

import numpy as np
import jax
import jax.numpy as jnp
from jax import lax
from jax.experimental import pallas as pl
from jax.experimental.pallas import tpu as pltpu

F32 = jnp.float32
BF16 = jnp.bfloat16

D_MODEL = 2048
DEPTH = 4
LN_EPS = 1e-5
ALPHA = (2 * DEPTH) ** 0.25

DIFF_HEADS = 16
DIFF_HEAD_DIM = 64
FOX_HEADS = 16
FOX_HEAD_DIM = 128
SWA_Q_HEADS = 32
SWA_KV_HEADS = 4
SWA_HEAD_DIM = 64
SWA_BLOCK = 128
SWA_Q_DIM = SWA_Q_HEADS * SWA_HEAD_DIM
SWA_KV_DIM = SWA_KV_HEADS * SWA_HEAD_DIM
N_EXPERTS = 8
TOP_K = 2

LANES = 128
VMEM_LIMIT = 56 * 1024 * 1024

NEG_INF = float("-inf")
_NT = (((1,), (1,)), ((), ()))


def _slopes_np(n):
    def pow2(m):
        start = 2.0 ** (-8.0 / m)
        return [start ** (i + 1) for i in range(m)]
    if n & (n - 1) == 0:
        s = pow2(n)
    else:
        c = 2 ** int(math.floor(math.log2(n)))
        s = pow2(c) + pow2(2 * c)[0::2][: n - c]
    return np.array(s, dtype=np.float32)


def _slopes(n):
    return jnp.asarray(_slopes_np(n))


def _params(sem, vmem=VMEM_LIMIT):
    return pltpu.CompilerParams(dimension_semantics=sem, vmem_limit_bytes=vmem)


def _layer_norm_rows(y, g, b):
    mu = jnp.mean(y, axis=-1, keepdims=True)
    yc = y - mu
    var = jnp.mean(yc * yc, axis=-1, keepdims=True)
    return yc * lax.rsqrt(var + LN_EPS) * g + b


def _sigmoid(x):
    return 1.0 / (1.0 + jnp.exp(-x))


def _proj_kernel(a_ref, w_ref, g_ref, o_ref, *, n_norm_tiles, tn):
    acc = jnp.dot(a_ref[...], w_ref[...].astype(BF16), preferred_element_type=F32)
    if n_norm_tiles == 0:
        o_ref[...] = acc.astype(o_ref.dtype)
        return
    j = pl.program_id(1)

    @pl.when(j < n_norm_tiles)
    def _():
        for c in range(tn // LANES):
            sl = slice(c * LANES, (c + 1) * LANES)
            blk = acc[:, sl]
            ms = jnp.mean(blk * blk, axis=-1, keepdims=True)
            o_ref[:, sl] = (blk * lax.rsqrt(ms + LN_EPS) * g_ref[:, sl]).astype(o_ref.dtype)

    @pl.when(j >= n_norm_tiles)
    def _():
        o_ref[...] = acc.astype(o_ref.dtype)


def _proj(a, w, layer, n_out, *, gain=None, n_norm=0, out_dtype=BF16, tm=1024, tn=512):
    m, k = a.shape
    tm = min(tm, m)
    tn = min(tn, n_out)
    if gain is None:
        gain = jnp.ones((1, n_out), F32)
    kern = functools.partial(_proj_kernel, n_norm_tiles=n_norm // tn, tn=tn)
    return pl.pallas_call(
        kern,
        grid=(m // tm, n_out // tn),
        in_specs=[pl.BlockSpec((tm, k), lambda i, j: (i, 0)),
                  pl.BlockSpec((None, k, tn), lambda i, j: (layer, 0, j)),
                  pl.BlockSpec((1, tn), lambda i, j: (0, j))],
        out_specs=pl.BlockSpec((tm, tn), lambda i, j: (i, j)),
        out_shape=jax.ShapeDtypeStruct((m, n_out), out_dtype),
        compiler_params=_params(("parallel", "arbitrary")),
    )(a, w, gain)


def _outproj_ln_kernel(o_ref, w_ref, h_ref, g_ref, b_ref, hf_ref, hb_ref, acc_ref):
    k = pl.program_id(1)

    @pl.when(k == 0)
    def _():
        acc_ref[...] = jnp.zeros_like(acc_ref)

    acc_ref[...] += jnp.dot(o_ref[...], w_ref[...].astype(BF16), preferred_element_type=F32)

    @pl.when(k == pl.num_programs(1) - 1)
    def _():
        out = _layer_norm_rows(ALPHA * h_ref[...] + acc_ref[...], g_ref[...], b_ref[...])
        hf_ref[...] = out
        hb_ref[...] = out.astype(BF16)


def _outproj_ln(o, w, layer, h, g, b, *, tm=512, tk=1024):
    m, k = o.shape
    n = w.shape[2]
    tm = min(tm, m)
    return pl.pallas_call(
        _outproj_ln_kernel,
        grid=(m // tm, k // tk),
        in_specs=[pl.BlockSpec((tm, tk), lambda i, kk: (i, kk)),
                  pl.BlockSpec((None, tk, n), lambda i, kk: (layer, kk, 0)),
                  pl.BlockSpec((tm, n), lambda i, kk: (i, 0)),
                  pl.BlockSpec((1, n), lambda i, kk: (0, 0)),
                  pl.BlockSpec((1, n), lambda i, kk: (0, 0))],
        out_specs=[pl.BlockSpec((tm, n), lambda i, kk: (i, 0)),
                   pl.BlockSpec((tm, n), lambda i, kk: (i, 0))],
        out_shape=[jax.ShapeDtypeStruct((m, n), F32), jax.ShapeDtypeStruct((m, n), BF16)],
        scratch_shapes=[pltpu.VMEM((tm, n), F32)],
        compiler_params=_params(("parallel", "arbitrary")),
    )(o, w, h, g, b)


def _res_ln_kernel(y_ref, h_ref, g_ref, b_ref, hf_ref, hb_ref):
    out = _layer_norm_rows(ALPHA * h_ref[...] + y_ref[...], g_ref[...], b_ref[...])
    hf_ref[...] = out
    hb_ref[...] = out.astype(BF16)


def _res_ln(y, h, g, b, *, tm=512):
    m, n = h.shape
    tm = min(tm, m)
    row = pl.BlockSpec((tm, n), lambda i: (i, 0))
    vec = pl.BlockSpec((1, n), lambda i: (0, 0))
    return pl.pallas_call(
        _res_ln_kernel,
        grid=(m // tm,),
        in_specs=[row, row, vec, vec],
        out_specs=[row, row],
        out_shape=[jax.ShapeDtypeStruct((m, n), F32), jax.ShapeDtypeStruct((m, n), BF16)],
        compiler_params=_params(("parallel",)),
    )(y, h, g, b)


ATTN_TILE = 512


def _split3(x):
    hi = x.astype(BF16).astype(F32)
    r = x - hi
    mid = r.astype(BF16).astype(F32)
    lo = (r - mid).astype(BF16).astype(F32)
    return hi, mid, lo


def _lane_groups(lane, groups):
    out = jnp.zeros(lane.shape, F32)
    for g, parts in enumerate(groups):
        for i, part in enumerate(parts):
            out = jnp.where(lane == 3 * g + i, part, out)
    return out


def _alibi_lane_tables(slopes, tq):
    def split3(x):
        hi = x.astype(BF16).astype(np.float32)
        mid = (x - hi).astype(BF16).astype(np.float32)
        lo = (x - hi - mid).astype(BF16).astype(np.float32)
        return hi, mid, lo

    slope = np.asarray(slopes, np.float32).reshape(-1, 1)
    row = np.arange(tq, dtype=np.float32).reshape(1, -1)
    q_tab = np.zeros((slope.shape[0], tq, LANES), np.float32)
    for i, part in enumerate(split3(slope)):
        q_tab[:, :, i] = part
        q_tab[:, :, 3 + i] = 256.0 * part
        q_tab[:, :, 6 + i] = -part
    for i, part in enumerate(split3(-(slope * row))):
        q_tab[:, :, 9 + i] = part
    k_tab = np.zeros((tq, LANES), np.float32)
    k_tab[:, 0:3] = (np.arange(tq) & 255).astype(np.float32)[:, None]
    k_tab[:, 3:6] = (np.arange(tq) >> 8).astype(np.float32)[:, None]
    k_tab[:, 9:12] = 1.0
    return jnp.asarray(q_tab.reshape(-1, LANES), BF16), jnp.asarray(k_tab)


def _flash_schedule(qi, scores, absorb, sa_ref, sb_ref):
    scores(0, sa_ref)

    def pair(j, carry):
        scores(2 * j + 1, sb_ref)
        absorb(2 * j, sa_ref, False)
        scores(2 * j + 2, sa_ref)
        absorb(2 * j + 1, sb_ref, False)
        return carry

    lax.fori_loop(0, qi // 2, pair, 0)

    @pl.when(qi % 2 == 1)
    def _():
        scores(qi, sb_ref)
        absorb(qi - 1, sa_ref, False)
        absorb(qi, sb_ref, True)

    @pl.when(qi % 2 == 0)
    def _():
        absorb(qi, sa_ref, True)


def _flash_absorb_t(st, v, m_ref, l_ref, acc_ref, c2):
    m = m_ref[...]
    m_new = jnp.maximum(m, jnp.max(st, axis=0, keepdims=True))
    alpha = jnp.exp2((m - m_new) * c2)
    p = jnp.exp2((st - m_new) * c2)
    l_ref[...] = alpha * l_ref[...] + jnp.sum(p, axis=0, keepdims=True)
    pv = lax.dot_general(v, p.astype(BF16), (((0,), (0,)), ((), ())), preferred_element_type=F32)
    acc_ref[...] = alpha * acc_ref[...] + pv
    m_ref[...] = m_new


def _flash_scratch_t(tk, cols):
    return [pltpu.VMEM((tk, cols), F32), pltpu.VMEM((tk, cols), F32),
            pltpu.VMEM((1, cols), F32), pltpu.VMEM((1, cols), F32),
            pltpu.VMEM((LANES, cols), F32)]


def _diff_attn_kernel(q_ref, k_ref, v_ref, qb_ref, kb_ref, lam_ref, gain_ref, o_ref,
                      sa_ref, sb_ref, m_ref, l_ref, acc_ref, *, tq, lam_init):
    qi = pl.program_id(2)
    half = DIFF_HEAD_DIM
    lane = lax.broadcasted_iota(jnp.int32, (tq, LANES), 1)
    k_bias_const = kb_ref[...]
    off_lanes = (lane >= 6) & (lane < 9)

    q = q_ref[...] * jnp.asarray(DIFF_HEAD_DIM ** -0.5, BF16)
    zero = jnp.zeros_like(q)
    qb = qb_ref[...]
    qs = jnp.concatenate(
        [jnp.concatenate([jnp.where(lane < half, q, zero), qb], axis=1),
         jnp.concatenate([jnp.where(lane < half, zero, q), qb], axis=1)], axis=0)

    krow = lax.broadcasted_iota(jnp.int32, (tq, tq), 0)
    qcol = lax.broadcasted_iota(jnp.int32, (tq, tq), 1)
    causal = jnp.concatenate([qcol >= krow, qcol >= krow], axis=1)

    m_ref[...] = jnp.full(m_ref.shape, NEG_INF, F32)
    l_ref[...] = jnp.zeros(l_ref.shape, F32)
    acc_ref[...] = jnp.zeros(acc_ref.shape, F32)

    def scores(kb, s_ref):
        start = pl.multiple_of(kb * tq, tq)
        off = ((qi - kb) * tq).astype(F32)
        kb_bias = jnp.where(off_lanes, off, k_bias_const).astype(BF16)
        k = jnp.concatenate([k_ref[pl.ds(start, tq), :], kb_bias], axis=1)
        s_ref[...] = lax.dot_general(k, qs, _NT, preferred_element_type=F32)

    def absorb(kb, s_ref, diag):
        s = s_ref[...]
        if diag:
            s = jnp.where(causal, s, NEG_INF)
        v = v_ref[pl.ds(pl.multiple_of(kb * tq, tq), tq), :]
        _flash_absorb_t(s, v, m_ref, l_ref, acc_ref, math.log2(math.e))

    _flash_schedule(qi, scores, absorb, sa_ref, sb_ref)

    lp = lam_ref[...]
    lam = (jnp.exp(jnp.sum(lp[0:1] * lp[1:2], axis=-1, keepdims=True))
           - jnp.exp(jnp.sum(lp[2:3] * lp[3:4], axis=-1, keepdims=True)) + lam_init)
    o = acc_ref[...] / l_ref[...]
    a = o[:, :tq] - lam * o[:, tq:]
    ms = jnp.mean(a * a, axis=0, keepdims=True)
    y = a * lax.rsqrt(ms + LN_EPS) * gain_ref[...] * (1.0 - lam_init)
    o_ref[...] = y.T.astype(o_ref.dtype)


def _diff_attention(proj, lam_params, gain, batch, seq, layer_idx, *, tq=ATTN_TILE):
    t = proj.shape[0]
    tq = min(tq, seq)
    nq = seq // tq
    nh = DIFF_HEADS
    lam_init = 0.8 - 0.6 * math.exp(-0.3 * layer_idx)
    kern = functools.partial(_diff_attn_kernel, tq=tq, lam_init=lam_init)
    q_tab, k_tab = _alibi_lane_tables(_slopes_np(nh), tq)
    return pl.pallas_call(
        kern,
        grid=(batch, nh, nq),
        in_specs=[pl.BlockSpec((tq, LANES), lambda b, h, qi: (b * nq + qi, h)),
                  pl.BlockSpec((seq, LANES), lambda b, h, qi: (b, nh + h)),
                  pl.BlockSpec((seq, LANES), lambda b, h, qi: (b, 2 * nh + h)),
                  pl.BlockSpec((tq, LANES), lambda b, h, qi: (h, 0)),
                  pl.BlockSpec((tq, LANES), lambda b, h, qi: (0, 0)),
                  pl.BlockSpec((4, DIFF_HEAD_DIM), lambda b, h, qi: (0, 0)),
                  pl.BlockSpec((LANES, 1), lambda b, h, qi: (0, 0))],
        out_specs=pl.BlockSpec((tq, LANES), lambda b, h, qi: (b * nq + qi, h)),
        out_shape=jax.ShapeDtypeStruct((t, D_MODEL), BF16),
        scratch_shapes=_flash_scratch_t(tq, 2 * tq),
        compiler_params=_params(("parallel", "parallel", "arbitrary")),
    )(proj, proj, proj, q_tab, k_tab, lam_params.astype(F32), gain.astype(F32).reshape(LANES, 1))


def _fox_c_kernel(h_ref, wf_ref, bf_ref, ccol_ref, carry_ref, *, tb):
    j = pl.program_id(1)

    @pl.when(j == 0)
    def _():
        carry_ref[...] = jnp.zeros_like(carry_ref)

    x = lax.dot_general(wf_ref[...].astype(BF16), h_ref[...], _NT, preferred_element_type=F32)
    x = x + bf_ref[...]
    lf = jnp.minimum(x, 0.0) - jnp.log1p(jnp.exp(-jnp.abs(x)))
    ii = lax.broadcasted_iota(jnp.int32, (tb, tb), 0)
    jj = lax.broadcasted_iota(jnp.int32, (tb, tb), 1)
    tri = jnp.where(ii <= jj, 1.0, 0.0).astype(BF16)
    pre = sum(jnp.dot(part.astype(BF16), tri, preferred_element_type=F32) for part in _split3(lf))
    c = pre + carry_ref[...]
    ccol_ref[...] = c.T
    carry_ref[...] = c[:, tb - 1:tb]


def _fox_cumlog(hb, wf_t, bf_col, batch, seq, *, tb=512):
    t, d = hb.shape
    tb = min(tb, seq)
    nb = seq // tb
    kern = functools.partial(_fox_c_kernel, tb=tb)
    return pl.pallas_call(
        kern,
        grid=(batch, nb),
        in_specs=[pl.BlockSpec((tb, d), lambda b, j: (b * nb + j, 0)),
                  pl.BlockSpec((LANES, d), lambda b, j: (0, 0)),
                  pl.BlockSpec((LANES, 1), lambda b, j: (0, 0))],
        out_specs=pl.BlockSpec((tb, LANES), lambda b, j: (b * nb + j, 0)),
        out_shape=jax.ShapeDtypeStruct((t, LANES), F32),
        scratch_shapes=[pltpu.VMEM((LANES, 1), F32)],
        compiler_params=_params(("parallel", "arbitrary")),
    )(hb, wf_t, bf_col)


def _fox_attn_kernel(q_ref, k_ref, v_ref, cq_ref, ck_ref, g_ref, o_ref,
                     sa_ref, sb_ref, m_ref, l_ref, acc_ref, *, tq):
    h = pl.program_id(1)
    qi = pl.program_id(2)
    scale = FOX_HEAD_DIM ** -0.5
    c2 = scale * math.log2(math.e)

    lane = lax.broadcasted_iota(jnp.int32, (tq, LANES), 1)
    head = lane == h

    def head_col(x):
        return jnp.sum(jnp.where(head, x, 0.0), axis=1, keepdims=True)

    ci = head_col(cq_ref[...])
    cref = ci[0:1, :]
    ones = (1.0, 1.0, 1.0)
    q_bias = _lane_groups(lane, [_split3(jnp.broadcast_to((ci - cref) * (1.0 / scale), (tq, LANES))), ones])
    qa = jnp.concatenate([q_ref[...], q_bias.astype(BF16)], axis=1)

    krow = lax.broadcasted_iota(jnp.int32, (tq, tq), 0)
    qcol = lax.broadcasted_iota(jnp.int32, (tq, tq), 1)
    causal = qcol >= krow

    m_ref[...] = jnp.full(m_ref.shape, NEG_INF, F32)
    l_ref[...] = jnp.zeros(l_ref.shape, F32)
    acc_ref[...] = jnp.zeros(acc_ref.shape, F32)

    def scores(kb, s_ref):
        start = pl.multiple_of(kb * tq, tq)
        cj = head_col(ck_ref[pl.ds(start, tq), :])
        neg = jnp.broadcast_to((cref - cj) * (1.0 / scale), (tq, LANES))
        k_bias = _lane_groups(lane, [ones, _split3(neg)])
        k = jnp.concatenate([k_ref[pl.ds(start, tq), :], k_bias.astype(BF16)], axis=1)
        s_ref[...] = lax.dot_general(k, qa, _NT, preferred_element_type=F32)

    def absorb(kb, s_ref, diag):
        s = s_ref[...]
        if diag:
            s = jnp.where(causal, s, NEG_INF)
        v = v_ref[pl.ds(pl.multiple_of(kb * tq, tq), tq), :]
        _flash_absorb_t(s, v, m_ref, l_ref, acc_ref, c2)

    _flash_schedule(qi, scores, absorb, sa_ref, sb_ref)
    o_ref[...] = ((acc_ref[...] / l_ref[...]).T * _sigmoid(g_ref[...])).astype(o_ref.dtype)


def _fox_attention(proj, ccol, gate, batch, seq, *, tq=ATTN_TILE):
    t = proj.shape[0]
    tq = min(tq, seq)
    nq = seq // tq
    nh = FOX_HEADS
    kern = functools.partial(_fox_attn_kernel, tq=tq)
    return pl.pallas_call(
        kern,
        grid=(batch, nh, nq),
        in_specs=[pl.BlockSpec((tq, LANES), lambda b, h, qi: (b * nq + qi, h)),
                  pl.BlockSpec((seq, LANES), lambda b, h, qi: (b, nh + h)),
                  pl.BlockSpec((seq, LANES), lambda b, h, qi: (b, 2 * nh + h)),
                  pl.BlockSpec((tq, LANES), lambda b, h, qi: (b * nq + qi, 0)),
                  pl.BlockSpec((seq, LANES), lambda b, h, qi: (b, 0)),
                  pl.BlockSpec((tq, LANES), lambda b, h, qi: (b * nq + qi, h))],
        out_specs=pl.BlockSpec((tq, LANES), lambda b, h, qi: (b * nq + qi, h)),
        out_shape=jax.ShapeDtypeStruct((t, D_MODEL), BF16),
        scratch_shapes=_flash_scratch_t(tq, tq),
        compiler_params=_params(("parallel", "parallel", "arbitrary")),
    )(proj, proj, proj, ccol, ccol, gate)


def _swa_attn_kernel(sinks_ref, slopes_ref, q_ref, ko_ref, kp_ref, vo_ref, vp_ref, o_ref, *, tq):
    pair = pl.program_id(1)
    qi = pl.program_id(2)
    half = SWA_HEAD_DIM
    hk = pair // (SWA_Q_HEADS // SWA_KV_HEADS // 2)
    par = hk % 2
    tk = tq + SWA_BLOCK

    lane_k = lax.broadcasted_iota(jnp.int32, (tk, LANES), 1)
    keep = jnp.where(lane_k >= half, 1, 0) == par

    def dup(prev_ref, own_ref):
        x = jnp.concatenate([prev_ref[...], own_ref[...]], axis=0).astype(F32)
        return jnp.where(keep, x, pltpu.roll(x, half, 1)).astype(BF16)

    kd = dup(kp_ref, ko_ref)
    vd = dup(vp_ref, vo_ref)

    q = q_ref[...] * jnp.asarray(SWA_HEAD_DIM ** -0.5, BF16)
    lane = lax.broadcasted_iota(jnp.int32, (tq, LANES), 1)
    lo = lane < half
    zero = jnp.zeros_like(q)
    qs = jnp.concatenate([jnp.where(lo, q, zero), jnp.where(lo, zero, q)], axis=0)
    s = lax.dot_general(qs, kd, _NT, preferred_element_type=F32)

    r = lax.broadcasted_iota(jnp.int32, (tq, tk), 0)
    c = lax.broadcasted_iota(jnp.int32, (tq, tk), 1)
    dist = r - c + SWA_BLOCK
    valid = (dist >= 0) & (dist < SWA_BLOCK) & (qi * tq + r - dist >= 0)
    distf = dist.astype(F32)

    outs = []
    for hh in range(2):
        head = 2 * pair + hh
        sh = s[hh * tq:(hh + 1) * tq] - slopes_ref[head] * distf
        sh = jnp.where(valid, sh, NEG_INF)
        sink = sinks_ref[head]
        m = jnp.maximum(jnp.max(sh, axis=-1, keepdims=True), sink)
        p = jnp.exp(sh - m)
        l = jnp.sum(p, axis=-1, keepdims=True) + jnp.exp(sink - m)
        outs.append(jnp.dot(p.astype(BF16), vd, preferred_element_type=F32) / l)
    o_ref[...] = jnp.where(lo, outs[0], outs[1]).astype(o_ref.dtype)


def _swa_attention(proj, sinks, batch, seq, *, tq=256):
    t = proj.shape[0]
    tq = min(tq, seq)
    nq = seq // tq
    npairs = SWA_Q_HEADS // 2
    pairs_per_col = LANES // SWA_HEAD_DIM * (SWA_Q_HEADS // SWA_KV_HEADS) // 2
    kcol = SWA_Q_DIM // LANES
    vcol = (SWA_Q_DIM + SWA_KV_DIM) // LANES
    sub = tq // SWA_BLOCK

    def own(col):
        return lambda b, p, qi, *_: (b * nq + qi, col + p // pairs_per_col)

    def prev(col):
        return lambda b, p, qi, *_: (jnp.maximum((b * nq + qi) * sub - 1, 0), col + p // pairs_per_col)

    grid_spec = pltpu.PrefetchScalarGridSpec(
        num_scalar_prefetch=2,
        grid=(batch, npairs, nq),
        in_specs=[pl.BlockSpec((tq, LANES), lambda b, p, qi, *_: (b * nq + qi, p)),
                  pl.BlockSpec((tq, LANES), own(kcol)),
                  pl.BlockSpec((SWA_BLOCK, LANES), prev(kcol)),
                  pl.BlockSpec((tq, LANES), own(vcol)),
                  pl.BlockSpec((SWA_BLOCK, LANES), prev(vcol))],
        out_specs=pl.BlockSpec((tq, LANES), lambda b, p, qi, *_: (b * nq + qi, p)),
    )
    kern = functools.partial(_swa_attn_kernel, tq=tq)
    return pl.pallas_call(
        kern,
        grid_spec=grid_spec,
        out_shape=jax.ShapeDtypeStruct((t, SWA_Q_DIM), BF16),
        compiler_params=_params(("parallel", "parallel", "arbitrary")),
    )(sinks.astype(F32), _slopes(SWA_Q_HEADS), proj, proj, proj, proj, proj)


def _ffn_kernel(te_ref, nt_ref, x_ref, wg_ref, wu_ref, wd_ref, o_ref):
    t = pl.program_id(0)
    f = pl.program_id(1)
    live = t < nt_ref[0]

    @pl.when(f == 0)
    def _():
        o_ref[...] = jnp.zeros_like(o_ref)

    @pl.when(live)
    def _():
        x = x_ref[...]
        g = jnp.dot(x, wg_ref[0].astype(BF16), preferred_element_type=F32)
        u = jnp.dot(x, wu_ref[0].astype(BF16), preferred_element_type=F32)
        a = (g * _sigmoid(g) * u).astype(BF16)
        o_ref[...] += jnp.dot(a, wd_ref[0].astype(BF16), preferred_element_type=F32)


def _ffn(x, w_gate, w_up, w_down, tile_expert, n_tiles, *, tm, tf=256):
    p, d = x.shape
    n_exp, _, ff = w_gate.shape
    nf = ff // tf
    nt_max = p // tm

    def xmap(t, f, te, nt):
        return (jnp.minimum(t, nt[0] - 1), 0)

    def fcol(t, f, nt):
        return jnp.where(t < nt[0], f, nf - 1)

    grid_spec = pltpu.PrefetchScalarGridSpec(
        num_scalar_prefetch=2,
        grid=(nt_max, nf),
        in_specs=[pl.BlockSpec((tm, d), xmap),
                  pl.BlockSpec((1, d, tf), lambda t, f, te, nt: (te[t], 0, fcol(t, f, nt))),
                  pl.BlockSpec((1, d, tf), lambda t, f, te, nt: (te[t], 0, fcol(t, f, nt))),
                  pl.BlockSpec((1, tf, d), lambda t, f, te, nt: (te[t], fcol(t, f, nt), 0))],
        out_specs=pl.BlockSpec((tm, d), lambda t, f, te, nt: (t, 0)),
    )
    return pl.pallas_call(
        _ffn_kernel,
        grid_spec=grid_spec,
        out_shape=jax.ShapeDtypeStruct((p, d), F32),
        compiler_params=_params(("arbitrary", "arbitrary")),
    )(tile_expert, n_tiles, x, w_gate, w_up, w_down)


def _router_kernel(h_ref, r_ref, o_ref):
    def split(x):
        hi = x.astype(BF16)
        return hi, (x - hi.astype(F32)).astype(BF16)

    h_hi, h_lo = split(h_ref[...])
    r_hi, r_lo = split(r_ref[...])
    logits = (jnp.dot(h_hi, r_hi, preferred_element_type=F32)
              + jnp.dot(h_hi, r_lo, preferred_element_type=F32)
              + jnp.dot(h_lo, r_hi, preferred_element_type=F32))
    lane = lax.broadcasted_iota(jnp.int32, logits.shape, 1)
    l1 = jnp.where(lane < N_EXPERTS, logits, NEG_INF)
    m1 = jnp.max(l1, axis=-1, keepdims=True)
    i1 = jnp.min(jnp.where(l1 == m1, lane, LANES), axis=-1, keepdims=True)
    l2 = jnp.where(lane == i1, NEG_INF, l1)
    m2 = jnp.max(l2, axis=-1, keepdims=True)
    i2 = jnp.min(jnp.where(l2 == m2, lane, LANES), axis=-1, keepdims=True)
    e = jnp.exp(m2 - m1)
    g1 = 1.0 / (1.0 + e)
    g2 = e / (1.0 + e)
    out = jnp.where(lane == 0, i1.astype(F32),
                    jnp.where(lane == 1, i2.astype(F32),
                              jnp.where(lane == 2, g1, jnp.where(lane == 3, g2, 0.0))))
    o_ref[...] = out


def _router(h, router_padded, *, tm=512):
    m, d = h.shape
    tm = min(tm, m)
    return pl.pallas_call(
        _router_kernel,
        grid=(m // tm,),
        in_specs=[pl.BlockSpec((tm, d), lambda i: (i, 0)),
                  pl.BlockSpec((d, LANES), lambda i: (0, 0))],
        out_specs=pl.BlockSpec((tm, LANES), lambda i: (i, 0)),
        out_shape=jax.ShapeDtypeStruct((m, LANES), F32),
        compiler_params=_params(("parallel",)),
    )(h, router_padded)


def _row_copy(src_hbm, row, dst, slot, sem):
    return pltpu.make_async_copy(src_hbm.at[pl.ds(row, 1), :], dst.at[pl.ds(slot, 1), :], sem)


def _double_buffered(issue, drain):
    i = pl.program_id(0)

    @pl.when(i == 0)
    def _():
        issue(0, 0)

    @pl.when(i + 1 < pl.num_programs(0))
    def _():
        issue(i + 1, (i + 1) % 2)

    drain(i % 2)
    return i % 2


def _gather_kernel(src_ref, h_hbm, o_ref, buf, sem, *, tg):
    def issue(tile, slot):
        def body(r, carry):
            _row_copy(h_hbm, src_ref[tile * tg + r], buf.at[slot], r, sem.at[slot]).start()
            return carry
        lax.fori_loop(0, tg, body, 0, unroll=8)

    def drain(slot):
        def body(r, carry):
            _row_copy(h_hbm, 0, buf.at[slot], r, sem.at[slot]).wait()
            return carry
        lax.fori_loop(0, tg, body, 0, unroll=8)

    slot = _double_buffered(issue, drain)
    o_ref[...] = buf[slot].astype(o_ref.dtype)


def _gather_rows(h, src, *, tg=512):
    p = src.shape[0]
    d = h.shape[1]
    kern = functools.partial(_gather_kernel, tg=tg)
    grid_spec = pltpu.PrefetchScalarGridSpec(
        num_scalar_prefetch=1,
        grid=(p // tg,),
        in_specs=[pl.BlockSpec(memory_space=pl.ANY)],
        out_specs=pl.BlockSpec((tg, d), lambda i, s: (i, 0)),
        scratch_shapes=[pltpu.VMEM((2, tg, d), F32), pltpu.SemaphoreType.DMA((2,))],
    )
    return pl.pallas_call(
        kern,
        grid_spec=grid_spec,
        out_shape=jax.ShapeDtypeStruct((p, d), BF16),
        compiler_params=_params(("arbitrary",)),
    )(src, h)


def _combine_ln_kernel(pos_ref, y_hbm, route_ref, h_ref, g_ref, b_ref, hf_ref, hb_ref,
                       buf0, buf1, sem, *, tm):
    def issue(tile, slot):
        def body(r, carry):
            tok = 2 * (tile * tm + r)
            _row_copy(y_hbm, pos_ref[tok], buf0.at[slot], r, sem.at[slot]).start()
            _row_copy(y_hbm, pos_ref[tok + 1], buf1.at[slot], r, sem.at[slot]).start()
            return carry
        lax.fori_loop(0, tm, body, 0, unroll=4)

    def drain(slot):
        def body(r, carry):
            _row_copy(y_hbm, 0, buf0.at[slot], r, sem.at[slot]).wait()
            _row_copy(y_hbm, 0, buf1.at[slot], r, sem.at[slot]).wait()
            return carry
        lax.fori_loop(0, tm, body, 0, unroll=4)

    slot = _double_buffered(issue, drain)
    route = route_ref[...]
    y = route[:, 2:3] * buf0[slot] + route[:, 3:4] * buf1[slot]
    out = _layer_norm_rows(ALPHA * h_ref[...] + y, g_ref[...], b_ref[...])
    hf_ref[...] = out
    hb_ref[...] = out.astype(BF16)


def _combine_ln(pos, ys, route, h, g, b, *, tm=256):
    m, d = h.shape
    tm = min(tm, m)
    kern = functools.partial(_combine_ln_kernel, tm=tm)
    row = pl.BlockSpec((tm, d), lambda i, s: (i, 0))
    vec = pl.BlockSpec((1, d), lambda i, s: (0, 0))
    grid_spec = pltpu.PrefetchScalarGridSpec(
        num_scalar_prefetch=1,
        grid=(m // tm,),
        in_specs=[pl.BlockSpec(memory_space=pl.ANY),
                  pl.BlockSpec((tm, LANES), lambda i, s: (i, 0)),
                  row, vec, vec],
        out_specs=[row, row],
        scratch_shapes=[pltpu.VMEM((2, tm, d), F32), pltpu.VMEM((2, tm, d), F32),
                        pltpu.SemaphoreType.DMA((2,))],
    )
    return pl.pallas_call(
        kern,
        grid_spec=grid_spec,
        out_shape=[jax.ShapeDtypeStruct((m, d), F32), jax.ShapeDtypeStruct((m, d), BF16)],
        compiler_params=_params(("arbitrary",)),
    )(pos, ys, route, h, g, b)


def _moe_layer(hf, hb, router, w_gate, w_up, w_down, layer, g, b, *, tm):
    t, d = hf.shape
    tm = min(tm, t)
    route = _router(hf, jnp.pad(router.astype(F32), ((0, 0), (0, LANES - N_EXPERTS))))

    expert = route[:, :TOP_K].astype(jnp.int32).reshape(-1)
    onehot = (expert[:, None] == jnp.arange(N_EXPERTS, dtype=jnp.int32)[None]).astype(jnp.int32)
    csum = jnp.cumsum(onehot, axis=0)
    counts = csum[-1]
    rank = jnp.take_along_axis(csum, expert[:, None], axis=1)[:, 0] - 1
    tiles_per = (counts + tm - 1) // tm
    tile_end = jnp.cumsum(tiles_per)
    start = (tile_end - tiles_per) * tm
    pos = (start[expert] + rank).astype(jnp.int32)
    nt_max = (t * TOP_K) // tm + N_EXPERTS
    n_tiles = tile_end[-1].astype(jnp.int32)
    src = jnp.zeros((nt_max * tm,), jnp.int32).at[pos].set(
        jnp.arange(t * TOP_K, dtype=jnp.int32) // TOP_K)
    tidx = jnp.arange(nt_max, dtype=jnp.int32)
    tile_expert = jnp.searchsorted(tile_end, jnp.minimum(tidx, n_tiles - 1), side="right")
    tile_expert = (jnp.minimum(tile_expert, N_EXPERTS - 1) + layer * N_EXPERTS).astype(jnp.int32)

    xs = _gather_rows(hf, src)
    ys = _ffn(xs, w_gate, w_up, w_down, tile_expert, n_tiles.reshape(1), tm=tm)
    return _combine_ln(pos, ys, route, hf, g, b)


def _dense_layer(hf, hb, w_gate, w_up, w_down, layer, g, b, *, tm):
    t = hf.shape[0]
    tm = min(tm, t)
    nt = t // tm
    ys = _ffn(hb, w_gate, w_up, w_down,
              jnp.full((nt,), layer, jnp.int32), jnp.full((1,), nt, jnp.int32), tm=tm)
    return _res_ln(ys, hf, g, b)


def _to_bf16_kernel(x_ref, o_ref):
    o_ref[...] = x_ref[...].astype(BF16)


def _to_bf16(x, *, tm=512):
    m, n = x.shape
    tm = min(tm, m)
    spec = pl.BlockSpec((tm, n), lambda i: (i, 0))
    return pl.pallas_call(
        _to_bf16_kernel, grid=(m // tm,), in_specs=[spec], out_specs=spec,
        out_shape=jax.ShapeDtypeStruct((m, n), BF16),
        compiler_params=_params(("parallel",)),
    )(x)


def _row(v):
    return v.astype(F32).reshape(1, -1)


def kernel(x, ln_gain, ln_bias, diff_w_in, diff_lambda, diff_subln_gain, diff_w_out,
           fox_w_in, fox_b_f, fox_qk_gain, fox_w_out, swa_w_in, swa_sinks, swa_w_out,
           dense_w_gate, dense_w_up, dense_w_down, moe_router, moe_w_gate, moe_w_up,
           moe_w_down, *, ffn_tm=1024):
    batch, seq, d = x.shape
    t = batch * seq
    hf = x.reshape(t, d).astype(F32)
    hb = _to_bf16(hf)
    depth = ln_gain.shape[0]
    moe_w = [w.reshape((-1,) + w.shape[2:]) for w in (moe_w_gate, moe_w_up, moe_w_down)]
    for i in range(depth):
        kind, j = i % 3, i // 3
        if kind == 0:
            proj = _proj(hb, diff_w_in, j, 3 * D_MODEL)
            o = _diff_attention(proj, diff_lambda[j], diff_subln_gain[j], batch, seq, i)
            w_out = diff_w_out
        elif kind == 1:
            w_in = fox_w_in[j]
            qkv = 3 * D_MODEL
            gain = jnp.concatenate([jnp.tile(fox_qk_gain[j, 0].astype(F32), FOX_HEADS),
                                    jnp.tile(fox_qk_gain[j, 1].astype(F32), FOX_HEADS),
                                    jnp.ones((D_MODEL,), F32)]).reshape(1, qkv)
            proj = _proj(hb, fox_w_in, j, qkv, gain=gain, n_norm=2 * D_MODEL)
            gate = _proj(hb, w_in[None, :, qkv + FOX_HEADS:], 0, D_MODEL, out_dtype=F32)
            wf_t = jnp.pad(w_in[:, qkv:qkv + FOX_HEADS].T, ((0, LANES - FOX_HEADS), (0, 0)))
            bf_col = jnp.pad(fox_b_f[j].astype(F32), (0, LANES - FOX_HEADS)).reshape(LANES, 1)
            ccol = _fox_cumlog(hb, wf_t, bf_col, batch, seq)
            o = _fox_attention(proj, ccol, gate, batch, seq)
            w_out = fox_w_out
        else:
            proj = _proj(hb, swa_w_in, j, SWA_Q_DIM + 2 * SWA_KV_DIM)
            o = _swa_attention(proj, swa_sinks[j], batch, seq)
            w_out = swa_w_out
        hf, hb = _outproj_ln(o, w_out, j, hf, _row(ln_gain[i, 0]), _row(ln_bias[i, 0]))
        c = i // 2
        g, b = _row(ln_gain[i, 1]), _row(ln_bias[i, 1])
        if i % 2 == 0:
            hf, hb = _dense_layer(hf, hb, dense_w_gate, dense_w_up, dense_w_down, c, g, b, tm=ffn_tm)
        else:
            hf, hb = _moe_layer(hf, hb, moe_router[c], *moe_w, c, g, b, tm=ffn_tm)
    return hf.reshape(batch, seq, d)
```
